```python
import math
import jax
import jax.numpy as jnp
from jax import lax
import numpy as np

D_MODEL = 4096
BATCH = 4
SEQ = 4096
DEPTH = 2

CTX_LEN = 256
GRID_W = 64
NORM_EPS = 1e-6
ROPE_BASE = 10000.0
CHUNK = 64
Q_BLOCK = 128
N_BRANCH = 3
BRANCH_W = 1024

HG_HEADS = 8
HG_DK = 128
HG_DV = 128
DA_HEADS = 8
DA_DH = 64
RT_HEADS = 8
RT_DK = 64
RT_DV = 128

N_EXPERTS = 64
TOP_K = 8
N_GROUPS = 8
TOPK_GROUPS = 4
EXPERT_FF = 256
SHARED_FF = 1024
ROUTED_SCALE = 2.5
MOE_BLOCK = 128

IN_SIZES = (HG_HEADS * HG_DK, HG_HEADS * HG_DK, HG_HEADS * HG_DK, HG_HEADS * HG_DV, HG_HEADS * HG_DV,
            2 * DA_HEADS * DA_DH, 2 * DA_HEADS * DA_DH, 2 * DA_HEADS * DA_DH,
            RT_HEADS * RT_DK, RT_HEADS * RT_DK, RT_HEADS * RT_DV, RT_HEADS * RT_DV)
IN_W = sum(IN_SIZES)

kernel_name = "hybrid_hgrn2_diffattn_retention_moe_dit"


def rms_norm(x, w=None):
    xf = x.astype(jnp.float32)
    y = xf * lax.rsqrt(jnp.mean(xf * xf, axis=-1, keepdims=True) + NORM_EPS)
    if w is not None:
        y = y * w.astype(jnp.float32)
    return y.astype(x.dtype)


def split_proj(p):
    offs = []
    acc = 0
    for s in IN_SIZES[:-1]:
        acc += s
        offs.append(acc)
    return jnp.split(p, offs, axis=-1)


def split_heads(t, n_heads):
    b, l, w = t.shape
    return t.reshape(b, l, n_heads, w // n_heads).transpose(0, 2, 1, 3)


def merge_heads(t):
    b, h, l, d = t.shape
    return t.transpose(0, 2, 1, 3).reshape(b, l, h * d)


def rotate(x, ang):
    x1, x2 = jnp.split(x, 2, axis=-1)
    cos = jnp.cos(ang).astype(x.dtype)
    sin = jnp.sin(ang).astype(x.dtype)
    return jnp.concatenate([x1 * cos - x2 * sin, x1 * sin + x2 * cos], axis=-1)


def axial_rope(x, rows, cols):
    half = x.shape[-1] // 2
    inv = ROPE_BASE ** (-jnp.arange(0, half, 2, dtype=jnp.float32) / half)
    ang_r = rows.astype(jnp.float32)[:, None] * inv[None, :]
    ang_c = cols.astype(jnp.float32)[:, None] * inv[None, :]
    return jnp.concatenate([rotate(x[..., :half], ang_r), rotate(x[..., half:], ang_c)], axis=-1)


def retnet_rotate(x, pos):
    d = x.shape[-1]
    theta = 1.0 / (ROPE_BASE ** jnp.linspace(0.0, 1.0, d // 2, dtype=jnp.float32))
    return rotate(x, pos.astype(jnp.float32)[:, None] * theta[None, :])


def to_chunks(t):
    b, h, l, d = t.shape
    return jnp.moveaxis(t.reshape(b, h, l // CHUNK, CHUNK, d), 2, 0)


def from_chunks(t):
    n, b, h, c, d = t.shape
    return jnp.moveaxis(t, 0, 2).reshape(b, h, n * c, d)


def hgrn2_chunk_scan(q, k, v, log_f, s0, emit_out):
    causal = jnp.tril(jnp.ones((CHUNK, CHUNK), dtype=bool))

    def step(s, blk):
        qb, kb, vb, gb = blk
        b = jnp.cumsum(gb, axis=2)
        b_end = b[:, :, -1:, :]
        s_new = (jnp.exp(b_end[:, :, 0, :, None]) * s
                 + jnp.einsum('bhsk,bhsv->bhkv', kb * jnp.exp(b_end - b), vb))
        if not emit_out:
            return s_new, None
        rel = jnp.where(causal[None, None, :, :, None],
                        b[:, :, :, None, :] - b[:, :, None, :, :], -jnp.inf)
        a = jnp.einsum('bhtk,bhsk,bhtsk->bhts', qb, kb, jnp.exp(rel))
        o = (jnp.einsum('bhts,bhsv->bhtv', a, vb)
             + jnp.einsum('bhtk,bhkv->bhtv', qb * jnp.exp(b), s))
        return s_new, o

    s_fin, o = lax.scan(step, s0, tuple(to_chunks(t) for t in (q, k, v, log_f)))
    return (from_chunks(o) if emit_out else None), s_fin


def retention_chunk_scan(q, k, v, s0, emit_out):
    log_gamma = jnp.log(1.0 - 2.0 ** (-5.0 - jnp.arange(RT_HEADS, dtype=jnp.float32)))
    idx = jnp.arange(CHUNK, dtype=jnp.float32)
    dist = idx[:, None] - idx[None, :]
    decay = jnp.where(dist >= 0, jnp.exp(log_gamma[:, None, None] * jnp.maximum(dist, 0.0)), 0.0)
    q_decay = jnp.exp(log_gamma[:, None] * (idx + 1.0))[None, :, :, None]
    k_decay = jnp.exp(log_gamma[:, None] * (CHUNK - 1.0 - idx))[None, :, :, None]
    chunk_decay = jnp.exp(log_gamma * CHUNK)[None, :, None, None]

    def step(s, blk):
        qb, kb, vb = blk
        s_new = chunk_decay * s + jnp.einsum('bhsk,bhsv->bhkv', kb * k_decay, vb)
        if not emit_out:
            return s_new, None
        a = jnp.einsum('bhtk,bhsk->bhts', qb, kb) * decay[None]
        o = (jnp.einsum('bhts,bhsv->bhtv', a, vb)
             + jnp.einsum('bhtk,bhkv->bhtv', qb * q_decay, s))
        return s_new, o

    s_fin, o = lax.scan(step, s0, tuple(to_chunks(t) for t in (q, k, v)))
    return (from_chunks(o) if emit_out else None), s_fin


def bidirectional_scan(scan_fn, lat_fwd, lat_bwd, ctx_fwd, ctx_bwd, s0, need_ctx):
    def flip(t):
        return jnp.flip(t, axis=2)
    oc_f, sc_f = scan_fn(*ctx_fwd, s0, need_ctx)
    oc_b, sc_b = scan_fn(*[flip(t) for t in ctx_bwd], s0, need_ctx)
    o_f, _ = scan_fn(*lat_fwd, sc_f, True)
    o_b, _ = scan_fn(*[flip(t) for t in lat_bwd], sc_b, True)
    o_lat = o_f + flip(o_b)
    o_ctx = (oc_f + flip(oc_b)) if need_ctx else None
    return o_lat, o_ctx


def hgrn2_branch(lat, ctx, lb_fwd, lb_bwd, norm_w, need_ctx):
    def prep(t):
        q, f_fwd, f_bwd, i, g = t
        q = split_heads(jax.nn.silu(q), HG_HEADS)
        v = split_heads(i, HG_HEADS)

        def forget(f, lb):
            f = lb + (1.0 - lb) * jax.nn.sigmoid(f.astype(jnp.float32))
            f = split_heads(f, HG_HEADS)
            return 1.0 - f, jnp.log(f)
        k_f, lf_f = forget(f_fwd, lb_fwd)
        k_b, lf_b = forget(f_bwd, lb_bwd)
        return (q, k_f, v, lf_f), (q, k_b, v, lf_b), g

    lat_f, lat_b, g = prep(lat)
    ctx_f, ctx_b, g_c = prep(ctx)
    s0 = jnp.zeros((lat[0].shape[0], HG_HEADS, HG_DK, HG_DV), jnp.float32)
    o_lat, o_ctx = bidirectional_scan(hgrn2_chunk_scan, lat_f, lat_b, ctx_f, ctx_b, s0, need_ctx)

    def readout(o, gate):
        return (merge_heads(rms_norm(o, norm_w)) * jax.nn.silu(gate.astype(jnp.float32))).astype(gate.dtype)
    return readout(o_lat, g), (readout(o_ctx, g_c) if need_ctx else None)


def diff_attention(q, k, v, lam, subln_w, lambda_init):
    b, h2, lq, dh = q.shape
    lk = k.shape[2]
    n_blk = lq // Q_BLOCK
    q_blocks = jnp.moveaxis(q.reshape(b, h2, n_blk, Q_BLOCK, dh), 2, 0)

    def one_block(qb):
        s = jnp.einsum('bhqd,bhkd->bhqk', qb, k).astype(jnp.float32)
        p = jax.nn.softmax(s, axis=-1).reshape(b, h2 // 2, 2, Q_BLOCK, lk)
        w = p[:, :, 0] - lam * p[:, :, 1]
        return jnp.einsum('bhqk,bhkv->bhqv', w.astype(v.dtype), v)

    o = lax.map(one_block, q_blocks)
    o = jnp.moveaxis(o, 0, 2).reshape(b, h2 // 2, lq, v.shape[-1])
    return merge_heads(rms_norm(o, subln_w) * (1.0 - lambda_init))


def lambda_init_for(layer):
    return 0.8 - 0.6 * math.exp(-0.3 * layer)


def diff_attn_branch(lat, ctx, rows, cols, lam_vec, subln_w, lambda_init, need_ctx):
    scale = DA_DH ** -0.5
    q, k, v = lat
    q = axial_rope(split_heads(q, 2 * DA_HEADS), rows, cols) * scale
    k = axial_rope(split_heads(k, 2 * DA_HEADS), rows, cols)
    v = split_heads(v, DA_HEADS)
    qc, kc, vc = ctx
    kc = split_heads(kc, 2 * DA_HEADS)
    vc = split_heads(vc, DA_HEADS)
    lv = lam_vec.astype(jnp.float32)
    lam = jnp.exp(jnp.sum(lv[0] * lv[1])) - jnp.exp(jnp.sum(lv[2] * lv[3])) + lambda_init
    o_lat = diff_attention(q, jnp.concatenate([kc, k], axis=2), jnp.concatenate([vc, v], axis=2),
                           lam, subln_w, lambda_init)
    o_ctx = None
    if need_ctx:
        qc = split_heads(qc, 2 * DA_HEADS) * scale
        o_ctx = diff_attention(qc, kc, vc, lam, subln_w, lambda_init)
    return o_lat, o_ctx


def retention_branch(lat, ctx, pos, need_ctx):
    kscale = RT_DK ** -0.5
    q, k, v, g = lat
    q = retnet_rotate(split_heads(q, RT_HEADS), pos)
    k = retnet_rotate(split_heads(k, RT_HEADS), pos) * kscale
    lat_qkv = (q, k, split_heads(v, RT_HEADS))
    qc, kc, vc, g_c = ctx
    ctx_qkv = (split_heads(qc, RT_HEADS), split_heads(kc, RT_HEADS) * kscale, split_heads(vc, RT_HEADS))
    s0 = jnp.zeros((q.shape[0], RT_HEADS, RT_DK, RT_DV), jnp.float32)
    o_lat, o_ctx = bidirectional_scan(retention_chunk_scan, lat_qkv, lat_qkv, ctx_qkv, ctx_qkv, s0, need_ctx)

    def readout(o, gate):
        return (jax.nn.silu(gate.astype(jnp.float32)) * merge_heads(rms_norm(o))).astype(gate.dtype)
    return readout(o_lat, g), (readout(o_ctx, g_c) if need_ctx else None)


def merge_branches(h, ys, w_branch, w_mgate, b_mgate, w_out):
    terms = [jax.nn.sigmoid(h @ w_mgate[i] + b_mgate[i]) * (ys[i] @ w_branch[i]) for i in range(N_BRANCH)]
    return sum(terms[1:], terms[0]) @ w_out


def swiglu(x, wg, wu, wd):
    return (jax.nn.silu(x @ wg) * (x @ wu)) @ wd


def route(h, w_router, router_bias):
    t = h.shape[0]
    scores = jax.nn.sigmoid((h @ w_router).astype(jnp.float32))
    choice = scores + router_bias.astype(jnp.float32)
    per_group = N_EXPERTS // N_GROUPS
    group_score = lax.top_k(choice.reshape(t, N_GROUPS, per_group), 2)[0].sum(-1)
    _, top_groups = lax.top_k(group_score, TOPK_GROUPS)
    group_mask = jnp.any(top_groups[:, :, None] == jnp.arange(N_GROUPS)[None, None, :], axis=1)
    expert_mask = jnp.repeat(group_mask, per_group, axis=1)
    _, idx = lax.top_k(jnp.where(expert_mask, choice, -jnp.inf), TOP_K)
    w = jnp.take_along_axis(scores, idx, axis=1)
    w = w / jnp.sum(w, axis=-1, keepdims=True) * ROUTED_SCALE
    return idx, w


def moe_routed(h, idx, w, w_gate, w_up, w_down):
    t, d = h.shape
    a = t * TOP_K
    flat_e = idx.reshape(a)
    flat_tok = jnp.repeat(jnp.arange(t, dtype=jnp.int32), TOP_K)
    flat_w = w.reshape(a).astype(h.dtype)
    order = jnp.argsort(flat_e)
    e_sorted = flat_e[order]
    counts = jnp.bincount(flat_e, length=N_EXPERTS)
    starts = jnp.cumsum(counts) - counts
    padded = (counts + MOE_BLOCK - 1) // MOE_BLOCK * MOE_BLOCK
    padded_end = jnp.cumsum(padded)
    dest = (padded_end - padded)[e_sorted] + jnp.arange(a, dtype=jnp.int32) - starts[e_sorted]
    n_blocks = -(-a // MOE_BLOCK) + N_EXPERTS
    size = n_blocks * MOE_BLOCK
    buf_tok = jnp.full((size,), t, jnp.int32).at[dest].set(flat_tok[order])
    buf_w = jnp.zeros((size,), h.dtype).at[dest].set(flat_w[order])
    block_start = jnp.arange(n_blocks, dtype=jnp.int32) * MOE_BLOCK
    block_expert = jnp.minimum(jnp.searchsorted(padded_end, block_start, side='right'), N_EXPERTS - 1)
    h_pad = jnp.concatenate([h, jnp.zeros((1, d), h.dtype)], axis=0)

    def step(acc, blk):
        tok, wt, e = blk
        yb = swiglu(h_pad[tok], w_gate[e], w_up[e], w_down[e]) * wt[:, None]
        return acc.at[tok].add(yb), None

    acc, _ = lax.scan(step, jnp.zeros((t + 1, d), h.dtype),
                      (buf_tok.reshape(n_blocks, MOE_BLOCK), buf_w.reshape(n_blocks, MOE_BLOCK), block_expert))
    return acc[:t]


def moe_ffn(h, w_router, router_bias, w_gate, w_up, w_down, ws_gate, ws_up, ws_down):
    idx, w = route(h, w_router, router_bias)
    return moe_routed(h, idx, w, w_gate, w_up, w_down) + swiglu(h, ws_gate, ws_up, ws_down)


def setup_inputs(seed: int = 0) -> dict:
    key = jax.random.key(seed)
    ks = jax.random.split(key, 26)
    d = D_MODEL

    def nrm(k, shape, scale):
        return jax.random.normal(k, shape, jnp.float32) * scale

    return {
        "x": nrm(ks[0], (BATCH, SEQ, d), 1.0),
        "c": nrm(ks[1], (BATCH, d), 1.0),
        "ctx": nrm(ks[2], (BATCH, CTX_LEN, d), 1.0),
        "c_ctx": nrm(ks[3], (d,), 1.0),
        "w_ada": nrm(ks[4], (DEPTH, d, 6 * d), 0.5 * d ** -0.5),
        "b_ada": nrm(ks[5], (DEPTH, 6 * d), 0.02),
        "norm_mix": 1.0 + nrm(ks[6], (DEPTH, d), 0.01),
        "norm_ffn": 1.0 + nrm(ks[7], (DEPTH, d), 0.01),
        "w_in": nrm(ks[8], (DEPTH, d, IN_W), d ** -0.5),
        "hgrn_lb": nrm(ks[9], (2, DEPTH, HG_HEADS * HG_DK), 0.1),
        "hgrn_norm": 1.0 + nrm(ks[10], (DEPTH, HG_DV), 0.01),
        "diff_lambda": nrm(ks[11], (DEPTH, 4, DA_DH), 0.1),
        "diff_subln": 1.0 + nrm(ks[12], (DEPTH, 2 * DA_DH), 0.01),
        "w_branch": nrm(ks[13], (DEPTH, N_BRANCH, BRANCH_W, d), BRANCH_W ** -0.5),
        "w_mgate": nrm(ks[14], (DEPTH, N_BRANCH, d, d), d ** -0.5),
        "b_mgate": nrm(ks[15], (DEPTH, N_BRANCH, d), 0.02),
        "w_out": nrm(ks[16], (DEPTH, d, d), d ** -0.5),
        "w_router": nrm(ks[17], (DEPTH, d, N_EXPERTS), d ** -0.5),
        "router_bias": nrm(ks[18], (DEPTH, N_EXPERTS), 0.01),
        "w_exp_gate": nrm(ks[19], (DEPTH, N_EXPERTS, d, EXPERT_FF), d ** -0.5),
        "w_exp_up": nrm(ks[20], (DEPTH, N_EXPERTS, d, EXPERT_FF), d ** -0.5),
        "w_exp_down": nrm(ks[21], (DEPTH, N_EXPERTS, EXPERT_FF, d), EXPERT_FF ** -0.5),
        "w_sh_gate": nrm(ks[22], (DEPTH, d, SHARED_FF), d ** -0.5),
        "w_sh_up": nrm(ks[23], (DEPTH, d, SHARED_FF), d ** -0.5),
        "w_sh_down": nrm(ks[24], (DEPTH, SHARED_FF, d), SHARED_FF ** -0.5),
        "norm_final": 1.0 + nrm(ks[25], (d,), 0.01),
    }


def reference(x, c, ctx, c_ctx, w_ada, b_ada, norm_mix, norm_ffn, w_in, hgrn_lb, hgrn_norm,
              diff_lambda, diff_subln, w_branch, w_mgate, b_mgate, w_out, w_router, router_bias,
              w_exp_gate, w_exp_up, w_exp_down, w_sh_gate, w_sh_up, w_sh_down, norm_final):
    b, seq, d = x.shape
    n_ctx = ctx.shape[1]
    rows_n = seq // GRID_W
    rows = jnp.repeat(jnp.arange(rows_n, dtype=jnp.int32), GRID_W)
    cols = jnp.tile(jnp.arange(GRID_W, dtype=jnp.int32), rows_n)
    pos = jnp.arange(seq, dtype=jnp.int32)
    p_lb = jax.nn.softmax(hgrn_lb.astype(jnp.float32), axis=1)
    lower_bounds = jnp.cumsum(p_lb, axis=1) - p_lb[:, :1]

    x_lat, x_ctx = x, ctx
    for l in range(DEPTH):
        need_ctx = l < DEPTH - 1
        mod = (jax.nn.silu(c) @ w_ada[l] + b_ada[l])[:, None, :]
        mod_c = (jax.nn.silu(c_ctx) @ w_ada[l] + b_ada[l])[None, None, :]
        sh1, sc1, g1, sh2, sc2, g2 = jnp.split(mod, 6, axis=-1)
        sh1c, sc1c, g1c, sh2c, sc2c, g2c = jnp.split(mod_c, 6, axis=-1)

        h = rms_norm(x_lat, norm_mix[l]) * (1.0 + sc1) + sh1
        hc = rms_norm(x_ctx, norm_mix[l]) * (1.0 + sc1c) + sh1c
        p = split_proj(h @ w_in[l])
        pc = split_proj(hc @ w_in[l])
        y_hg, yc_hg = hgrn2_branch(p[0:5], pc[0:5], lower_bounds[0, l], lower_bounds[1, l],
                                   hgrn_norm[l], need_ctx)
        y_da, yc_da = diff_attn_branch(p[5:8], pc[5:8], rows, cols, diff_lambda[l], diff_subln[l],
                                       lambda_init_for(l), need_ctx)
        y_rt, yc_rt = retention_branch(p[8:12], pc[8:12], pos, need_ctx)
        x_lat = x_lat + g1 * merge_branches(h, (y_hg, y_da, y_rt), w_branch[l], w_mgate[l], b_mgate[l], w_out[l])
        if need_ctx:
            x_ctx = x_ctx + g1c * merge_branches(hc, (yc_hg, yc_da, yc_rt), w_branch[l], w_mgate[l],
                                                 b_mgate[l], w_out[l])

        h2 = rms_norm(x_lat, norm_ffn[l]) * (1.0 + sc2) + sh2
        tokens = h2.reshape(b * seq, d)
        if need_ctx:
            h2c = rms_norm(x_ctx, norm_ffn[l]) * (1.0 + sc2c) + sh2c
            tokens = jnp.concatenate([tokens, h2c.reshape(b * n_ctx, d)], axis=0)
        y = moe_ffn(tokens, w_router[l], router_bias[l], w_exp_gate[l], w_exp_up[l], w_exp_down[l],
                    w_sh_gate[l], w_sh_up[l], w_sh_down[l])
        x_lat = x_lat + g2 * y[:b * seq].reshape(b, seq, d)
        if need_ctx:
            x_ctx = x_ctx + g2c * y[b * seq:].reshape(b, n_ctx, d)

    return rms_norm(x_lat, norm_final)
```

```python
import functools
import math

import numpy as np
import jax
import jax.numpy as jnp
from jax import lax
from jax.experimental import pallas as pl
from jax.experimental.pallas import tpu as pltpu

F32 = jnp.float32
BF16 = jnp.bfloat16

NORM_EPS = 1e-6
ROPE_BASE = 10000.0
GRID_W = 64
HG_HEADS = 8
DA_HEADS = 8
DA_DH = 64
RT_HEADS = 8
RT_DK = 64
N_EXPERTS = 64
TOP_K = 8
N_GROUPS = 8
TOPK_GROUPS = 4
EXPERT_FF = 256
ROUTED_SCALE = 2.5
BRANCH_W = 1024

LANES = 128
V7X_VMEM_BYTES = 64 * 1024 * 1024
VMEM_LIMIT = 56 * 1024 * 1024

COL_HG_Q, COL_HG_FF, COL_HG_FB, COL_HG_I, COL_HG_G = 0, 8, 16, 24, 32
COL_DA_Q, COL_DA_K, COL_DA_V = 40, 48, 56
COL_RT_Q, COL_RT_K, COL_RT_V, COL_RT_G = 64, 68, 72, 80

NT_DIMS = (((1,), (1,)), ((), ()))
TN_DIMS = (((0,), (0,)), ((), ()))

ROW_TILE = 256
MM_TM = 1024
HG_CHUNK = 128
RT_CHUNK = 256
ATT_TQ = 256
ATT_TK = 512
MOE_BM = 256


def _cparams(*sem):
    return pltpu.CompilerParams(dimension_semantics=sem, vmem_limit_bytes=VMEM_LIMIT)


def _div_tile(n, pref):
    t = min(n, pref)
    while n % t:
        t //= 2
    return t


def _sigmoid(x):
    return 1.0 / (1.0 + jnp.exp(-x))


def _silu(x):
    return x * _sigmoid(x)


def _ada_kernel(c_ref, w_ref, b_ref, o_ref):
    a = _silu(c_ref[...]).astype(BF16)
    o_ref[...] = jnp.dot(a, w_ref[...].astype(BF16), preferred_element_type=F32) + b_ref[...]


def _ada_mod(c8, w_ada, b_ada):
    depth, d, n = w_ada.shape
    tn = _div_tile(n, 512)
    return pl.pallas_call(
        _ada_kernel,
        out_shape=jax.ShapeDtypeStruct((depth, 8, n), F32),
        grid=(depth, n // tn),
        in_specs=[
            pl.BlockSpec((8, d), lambda l, j: (0, 0)),
            pl.BlockSpec((None, d, tn), lambda l, j: (l, 0, j)),
            pl.BlockSpec((None, 1, tn), lambda l, j: (l, 0, j)),
        ],
        out_specs=pl.BlockSpec((None, 8, tn), lambda l, j: (l, 0, j)),
        compiler_params=_cparams("arbitrary", "arbitrary"),
        name="ada_mod",
    )(c8, w_ada, b_ada.reshape(depth, 1, n))


def _mod_row_map(tm, seq, n_lat, batch):
    def index_map(i):
        start = i * tm
        return (jnp.where(start < n_lat, start // seq, batch), 0, 0)
    return index_map


def _norm_mod_kernel(x_ref, w_ref, sc_ref, sh_ref, o_ref):
    x = x_ref[...]
    ms = jnp.mean(x * x, axis=-1, keepdims=True)
    y = x * lax.rsqrt(ms + NORM_EPS) * w_ref[...]
    o_ref[...] = (y * (1.0 + sc_ref[...]) + sh_ref[...]).astype(o_ref.dtype)


def _norm_mod(x, w, sc, sh, rows, seq, n_lat, batch, out_dtype):
    d = x.shape[1]
    tm = _div_tile(math.gcd(seq, rows), ROW_TILE)
    rmap = _mod_row_map(tm, seq, n_lat, batch)
    return pl.pallas_call(
        _norm_mod_kernel,
        out_shape=jax.ShapeDtypeStruct((rows, d), out_dtype),
        grid=(rows // tm,),
        in_specs=[
            pl.BlockSpec((tm, d), lambda i: (i, 0)),
            pl.BlockSpec((1, d), lambda i: (0, 0)),
            pl.BlockSpec((None, 1, d), rmap),
            pl.BlockSpec((None, 1, d), rmap),
        ],
        out_specs=pl.BlockSpec((tm, d), lambda i: (i, 0)),
        compiler_params=_cparams("parallel"),
        name="norm_mod",
    )(x, w.reshape(1, d), sc, sh)


def _mm_kernel(a_ref, b_ref, o_ref):
    o_ref[...] = jnp.dot(a_ref[...], b_ref[...], preferred_element_type=F32).astype(o_ref.dtype)


def _matmul(a, b, out_dtype, tn_pref=1024):
    m, k = a.shape
    n = b.shape[1]
    tm = _div_tile(m, MM_TM)
    tn = _div_tile(n, tn_pref)
    return pl.pallas_call(
        _mm_kernel,
        out_shape=jax.ShapeDtypeStruct((m, n), out_dtype),
        grid=(m // tm, n // tn),
        in_specs=[pl.BlockSpec((tm, k), lambda i, j: (i, 0)),
                  pl.BlockSpec((k, tn), lambda i, j: (0, j))],
        out_specs=pl.BlockSpec((tm, tn), lambda i, j: (i, j)),
        compiler_params=_cparams("parallel", "arbitrary"),
        name="matmul",
    )(a, b)


def _rope_tables(seq, n_ctx):
    pos = np.arange(seq, dtype=np.float64)
    rows = np.floor(pos / GRID_W)
    cols = pos - rows * GRID_W
    lane = np.arange(LANES)
    j = lane % 64
    part, jj = j // 32, j % 32
    inv = ROPE_BASE ** (-np.arange(0, 32, 2, dtype=np.float64) / 32.0)
    ang = np.where(part[None, :] == 0, rows[:, None], cols[:, None]) * inv[jj % 16][None, :]
    cos_da = np.cos(ang)
    sin_da = np.sin(ang) * np.where(jj < 16, -1.0, 1.0)[None, :]
    partner_da = np.where(jj < 16, lane + 16, lane - 16)
    theta = 1.0 / (ROPE_BASE ** np.linspace(0.0, 1.0, 32))
    ang = pos[:, None] * theta[j % 32][None, :]
    cos_rt = np.cos(ang)
    sin_rt = np.sin(ang) * np.where(j < 32, -1.0, 1.0)[None, :]
    partner_rt = np.where(j < 32, lane + 32, lane - 32)

    def full(tab, fill):
        return jnp.asarray(np.concatenate([tab, np.full((n_ctx, LANES), fill)], 0), F32)

    def perm(partner):
        p = np.zeros((LANES, LANES), np.float32)
        p[partner, lane] = 1.0
        return jnp.asarray(p, BF16)

    return (full(cos_da, 1.0), full(sin_da, 0.0), full(cos_rt, 1.0), full(sin_rt, 0.0),
            perm(partner_da), perm(partner_rt))


def _rope_kernel(qd_ref, kd_ref, qr_ref, kr_ref, cd_ref, sd_ref, cr_ref, sr_ref, pd_ref, pr_ref,
                 oqd_ref, okd_ref, oqr_ref, okr_ref):
    def rot(src, dst, cos_ref, sin_ref, perm_ref, scale):
        cos = cos_ref[...]
        sin = sin_ref[...]
        perm = perm_ref[...]
        for g in range(src.shape[1] // LANES):
            sl = slice(g * LANES, (g + 1) * LANES)
            x = src[:, sl]
            xs = jnp.dot(x, perm, preferred_element_type=F32)
            y = x.astype(F32) * cos + xs * sin
            dst[:, sl] = (y * scale).astype(dst.dtype)

    rot(qd_ref, oqd_ref, cd_ref, sd_ref, pd_ref, DA_DH ** -0.5)
    rot(kd_ref, okd_ref, cd_ref, sd_ref, pd_ref, 1.0)
    rot(qr_ref, oqr_ref, cr_ref, sr_ref, pr_ref, 1.0)
    rot(kr_ref, okr_ref, cr_ref, sr_ref, pr_ref, RT_DK ** -0.5)


def _rope_prep(p, tables, tm, seq, n_lat):
    rows = p.shape[0]
    cos_da, sin_da, cos_rt, sin_rt, perm_da, perm_rt = tables
    lat_tiles, seq_tiles = n_lat // tm, seq // tm

    def tmap(i):
        return (jnp.where(i < lat_tiles, i % seq_tiles, seq_tiles), 0)

    wd, wr = DA_HEADS * 2 * DA_DH, RT_HEADS * RT_DK
    tab = pl.BlockSpec((tm, LANES), tmap)
    pm = pl.BlockSpec((LANES, LANES), lambda i: (0, 0))
    return pl.pallas_call(
        _rope_kernel,
        out_shape=(jax.ShapeDtypeStruct((rows, wd), BF16), jax.ShapeDtypeStruct((rows, wd), BF16),
                   jax.ShapeDtypeStruct((rows, wr), BF16), jax.ShapeDtypeStruct((rows, wr), BF16)),
        grid=(rows // tm,),
        in_specs=[
            pl.BlockSpec((tm, wd), lambda i: (i, COL_DA_Q * LANES // wd)),
            pl.BlockSpec((tm, wd), lambda i: (i, COL_DA_K * LANES // wd)),
            pl.BlockSpec((tm, wr), lambda i: (i, COL_RT_Q * LANES // wr)),
            pl.BlockSpec((tm, wr), lambda i: (i, COL_RT_K * LANES // wr)),
            tab, tab, tab, tab, pm, pm,
        ],
        out_specs=(pl.BlockSpec((tm, wd), lambda i: (i, 0)), pl.BlockSpec((tm, wd), lambda i: (i, 0)),
                   pl.BlockSpec((tm, wr), lambda i: (i, 0)), pl.BlockSpec((tm, wr), lambda i: (i, 0))),
        compiler_params=_cparams("parallel"),
        name="rope_prep",
    )(p, p, p, p, cos_da, sin_da, cos_rt, sin_rt, perm_da, perm_rt)


def _hgrn_tables(c):
    nlev = int(math.log2(c))
    m = np.zeros((nlev + 2, c, c), np.float32)
    w = np.zeros((nlev + 1, c, c), np.float32)
    for l in range(nlev):
        bs = c >> (l + 1)
        for t in range(c):
            blk = t // bs
            if blk % 2 == 1:
                m[l, t, blk * bs:t + 1] = 1.0
                w[l, t, (blk - 1) * bs:blk * bs] = 1.0
            else:
                m[l, t, t + 1:(blk + 1) * bs] = 1.0
    for t in range(c):
        m[nlev, t, :t + 1] = 1.0
        m[nlev + 1, t, t + 1:] = 1.0
    w[nlev] = np.eye(c)
    mb = m[:, ::-1, ::-1]
    wb = w[:, ::-1, ::-1]
    r = (nlev + 2) * c
    return (jnp.asarray(m.reshape(r, c), BF16), jnp.asarray(mb.reshape(r, c), BF16),
            jnp.asarray(w, F32), jnp.asarray(wb, F32))


def _hgrn_chunk(q_raw, f_raw, v, lb, m_ref, w_ref, st, fwd):
    c = q_raw.shape[0]
    nlev = w_ref.shape[0] - 1
    q = _silu(q_raw.astype(F32))
    f = lb + (1.0 - lb) * _sigmoid(f_raw.astype(F32))
    lf = jnp.log(f)
    k = 1.0 - f
    hi = lf.astype(BF16)
    lo = (lf - hi.astype(F32)).astype(BF16)
    g2 = jnp.dot(m_ref[...], jnp.concatenate([hi, lo], axis=1), preferred_element_type=F32)
    g = g2[:, :LANES] + g2[:, LANES:]
    e = jnp.exp(g)
    row = lax.broadcasted_iota(jnp.int32, (c, LANES), 0)
    a = w_ref[nlev] * lax.dot_general(q.astype(BF16), k.astype(BF16), NT_DIMS,
                                      preferred_element_type=F32)
    for l in range(nlev):
        shift = int(math.log2(c >> (l + 1)))
        odd = ((row >> shift) & 1) == 1
        x = (jnp.where(odd, q, k) if fwd else jnp.where(odd, k, q)) * e[l * c:(l + 1) * c]
        xb = x.astype(BF16)
        a = a + w_ref[l] * lax.dot_general(xb, xb, NT_DIMS, preferred_element_type=F32)
    qe = (q * e[nlev * c:(nlev + 1) * c]).astype(BF16)
    ke = (k * e[(nlev + 1) * c:(nlev + 2) * c]).astype(BF16)
    o = (jnp.dot(a.astype(BF16), v, preferred_element_type=F32)
         + lax.dot_general(qe, st.astype(BF16), NT_DIMS, preferred_element_type=F32))
    last = nlev * c + (c - 1 if fwd else 0)
    dec = jnp.exp(g[last:last + 1, :])
    st_new = dec * st + lax.dot_general(v, ke, TN_DIMS, preferred_element_type=F32)
    return o, st_new


def _hgrn_kernel(need_ctx, chunk, *refs):
    (ql, ffl, fbl, il, gl, qc, ffc, fbc, ic, gc, lbf_ref, lbb_ref, nw_ref,
     mf_ref, mb_ref, wf_ref, wb_ref) = refs[:17]
    rest = refs[17:]
    if need_ctx:
        yl_ref, yc_ref, of_ref, ob_ref, ocf_ref, ocb_ref, st_ref = rest
    else:
        yl_ref, of_ref, ob_ref, st_ref = rest
        yc_ref = ocf_ref = ocb_ref = None
    nc_lat = ql.shape[0] // chunk
    nc_ctx = qc.shape[0] // chunk
    lbf = lbf_ref[...]
    lbb = lbb_ref[...]
    st_ref[...] = jnp.zeros_like(st_ref)

    def step(j, q_ref, ff_ref, fb_ref, i_ref, n, outf, outb):
        sf = pl.ds(pl.multiple_of(j * chunk, chunk), chunk)
        sb = pl.ds(pl.multiple_of((n - 1 - j) * chunk, chunk), chunk)
        o, s = _hgrn_chunk(q_ref[sf, :], ff_ref[sf, :], i_ref[sf, :], lbf, mf_ref, wf_ref,
                           st_ref[0], True)
        st_ref[0] = s
        if outf is not None:
            outf[sf, :] = o
        o, s = _hgrn_chunk(q_ref[sb, :], fb_ref[sb, :], i_ref[sb, :], lbb, mb_ref, wb_ref,
                           st_ref[1], False)
        st_ref[1] = s
        if outb is not None:
            outb[sb, :] = o

    def ctx_body(j, carry):
        step(j, qc, ffc, fbc, ic, nc_ctx, ocf_ref, ocb_ref)
        return carry

    def lat_body(j, carry):
        step(j, ql, ffl, fbl, il, nc_lat, of_ref, ob_ref)
        return carry

    lax.fori_loop(0, nc_ctx, ctx_body, 0)
    lax.fori_loop(0, nc_lat, lat_body, 0)

    nw = nw_ref[...]

    def readout(n, a_ref, b_ref, g_ref, y_ref):
        def body(j, carry):
            s = pl.ds(pl.multiple_of(j * chunk, chunk), chunk)
            o = a_ref[s, :] + b_ref[s, :]
            o = o * lax.rsqrt(jnp.mean(o * o, axis=-1, keepdims=True) + NORM_EPS) * nw
            y_ref[s, :] = (o * _silu(g_ref[s, :].astype(F32))).astype(y_ref.dtype)
            return carry
        lax.fori_loop(0, n, body, 0)

    readout(nc_lat, of_ref, ob_ref, gl, yl_ref)
    if need_ctx:
        readout(nc_ctx, ocf_ref, ocb_ref, gc, yc_ref)


def _hgrn(p, lb_f, lb_b, norm_w, tables, batch, seq, ctx_len, need_ctx):
    n_lat = batch * seq
    chunk = _div_tile(math.gcd(seq, ctx_len), HG_CHUNK)
    mf, mb, wf, wb = tables
    ctx_blk0 = n_lat // ctx_len
    width = HG_HEADS * LANES

    def lat(col):
        return pl.BlockSpec((seq, LANES), lambda b, h: (b, col + h))

    def ctx(col):
        return pl.BlockSpec((ctx_len, LANES), lambda b, h: (ctx_blk0 + b, col + h))

    head_vec = pl.BlockSpec((1, LANES), lambda b, h: (0, h))
    const2 = lambda arr: pl.BlockSpec(arr.shape, lambda b, h: (0,) * arr.ndim)
    cols = (COL_HG_Q, COL_HG_FF, COL_HG_FB, COL_HG_I, COL_HG_G)
    in_specs = ([lat(c) for c in cols] + [ctx(c) for c in cols]
                + [head_vec, head_vec, pl.BlockSpec((1, LANES), lambda b, h: (0, 0)),
                   const2(mf), const2(mb), const2(wf), const2(wb)])
    out_shape = [jax.ShapeDtypeStruct((n_lat, width), BF16)]
    out_specs = [pl.BlockSpec((seq, LANES), lambda b, h: (b, h))]
    scratch = [pltpu.VMEM((seq, LANES), F32), pltpu.VMEM((seq, LANES), F32)]
    if need_ctx:
        out_shape.append(jax.ShapeDtypeStruct((batch * ctx_len, width), BF16))
        out_specs.append(pl.BlockSpec((ctx_len, LANES), lambda b, h: (b, h)))
        scratch += [pltpu.VMEM((ctx_len, LANES), F32), pltpu.VMEM((ctx_len, LANES), F32)]
    scratch.append(pltpu.VMEM((2, LANES, LANES), F32))
    outs = pl.pallas_call(
        functools.partial(_hgrn_kernel, need_ctx, chunk),
        out_shape=tuple(out_shape),
        grid=(batch, HG_HEADS),
        in_specs=in_specs,
        out_specs=tuple(out_specs),
        scratch_shapes=scratch,
        compiler_params=_cparams("parallel", "parallel"),
        name="hgrn2_scan",
    )(*([p] * 10), lb_f.reshape(1, width), lb_b.reshape(1, width), norm_w.reshape(1, LANES),
      mf, mb, wf, wb)
    return outs[0], (outs[1] if need_ctx else None)


def _attn_kernel(has_lat, tk, lambda_init, *refs):
    if has_lat:
        q_ref, kc_ref, vc_ref, kl_ref, vl_ref, lam_ref, sub_ref, o_ref, s_ref = refs
    else:
        q_ref, kc_ref, vc_ref, lam_ref, sub_ref, o_ref = refs
    tq = q_ref.shape[0]
    q = q_ref[...].astype(F32)
    lane = lax.broadcasted_iota(jnp.int32, (tq, LANES), 1)
    qs = jnp.concatenate([jnp.where(lane < DA_DH, q, 0.0), jnp.where(lane >= DA_DH, q, 0.0)],
                         axis=0).astype(BF16)
    sc = lax.dot_general(qs, kc_ref[...], NT_DIMS, preferred_element_type=F32)
    m = jnp.max(sc, axis=-1, keepdims=True)
    if has_lat:
        nk = kl_ref.shape[0] // tk

        def scores(j, m):
            rows = pl.ds(pl.multiple_of(j * tk, tk), tk)
            s = lax.dot_general(qs, kl_ref[rows, :], NT_DIMS, preferred_element_type=F32)
            s_ref[j] = s
            return jnp.maximum(m, jnp.max(s, axis=-1, keepdims=True))

        m = lax.fori_loop(0, nk, scores, m)
    ec = jnp.exp(sc - m)
    l = jnp.sum(ec, axis=-1, keepdims=True)
    pv = jnp.dot(ec.astype(BF16), vc_ref[...], preferred_element_type=F32)
    if has_lat:
        def accumulate(j, carry):
            l, pv = carry
            rows = pl.ds(pl.multiple_of(j * tk, tk), tk)
            e = jnp.exp(s_ref[j] - m)
            l = l + jnp.sum(e, axis=-1, keepdims=True)
            pv = pv + jnp.dot(e.astype(BF16), vl_ref[rows, :], preferred_element_type=F32)
            return l, pv

        l, pv = lax.fori_loop(0, nk, accumulate, (l, pv))
    on = pv / l
    lv = lam_ref[...]
    lam = (jnp.exp(jnp.sum(lv[0:1] * lv[1:2], axis=-1, keepdims=True))
           - jnp.exp(jnp.sum(lv[2:3] * lv[3:4], axis=-1, keepdims=True)) + lambda_init)
    o = on[:tq] - lam * on[tq:]
    o = o * lax.rsqrt(jnp.mean(o * o, axis=-1, keepdims=True) + NORM_EPS) * sub_ref[...]
    o_ref[...] = (o * (1.0 - lambda_init)).astype(o_ref.dtype)


def _diff_attn(qd, kd, p, lam_vec, subln_w, lambda_init, batch, seq, ctx_len, latent_queries):
    n_lat = batch * seq
    ctx_blk0 = n_lat // ctx_len
    width = DA_HEADS * LANES
    lq = seq if latent_queries else ctx_len
    tq = _div_tile(lq, ATT_TQ)
    q_blk0 = 0 if latent_queries else n_lat // tq
    qt = lq // tq
    tk = _div_tile(seq, ATT_TK)
    in_specs = [
        pl.BlockSpec((tq, LANES), lambda b, h, i: (q_blk0 + b * qt + i, h)),
        pl.BlockSpec((ctx_len, LANES), lambda b, h, i: (ctx_blk0 + b, h)),
        pl.BlockSpec((ctx_len, LANES), lambda b, h, i: (ctx_blk0 + b, COL_DA_V + h)),
    ]
    args = [qd, kd, p]
    scratch = []
    if latent_queries:
        in_specs += [pl.BlockSpec((seq, LANES), lambda b, h, i: (b, h)),
                     pl.BlockSpec((seq, LANES), lambda b, h, i: (b, COL_DA_V + h))]
        args += [kd, p]
        scratch = [pltpu.VMEM((seq // tk, 2 * tq, tk), F32)]
    in_specs += [pl.BlockSpec((4, DA_DH), lambda b, h, i: (0, 0)),
                 pl.BlockSpec((1, LANES), lambda b, h, i: (0, 0))]
    args += [lam_vec, subln_w.reshape(1, LANES)]
    return pl.pallas_call(
        functools.partial(_attn_kernel, latent_queries, tk, lambda_init),
        out_shape=jax.ShapeDtypeStruct((batch * lq, width), BF16),
        grid=(batch, DA_HEADS, qt),
        in_specs=in_specs,
        out_specs=pl.BlockSpec((tq, LANES), lambda b, h, i: (b * qt + i, h)),
        scratch_shapes=scratch,
        compiler_params=_cparams("parallel", "parallel", "arbitrary"),
        name="diff_attn_lat" if latent_queries else "diff_attn_ctx",
    )(*args)


def _ret_tables(c):
    gam = 1.0 - 2.0 ** (-5.0 - np.arange(RT_HEADS, dtype=np.float64))
    idx = np.arange(c, dtype=np.float64)
    dist = np.abs(idx[:, None] - idx[None, :])
    dsym = gam[:, None, None] ** dist[None] * np.where(dist == 0, 2.0, 1.0)[None]
    lane_head = np.arange(LANES) // RT_DK
    mask = (lane_head[None, :] == (np.arange(RT_HEADS) % 2)[:, None]).astype(np.float64)

    def tab(power):
        return (gam[:, None] ** power[None, :])[:, :, None] * mask[:, None, :]

    qdf, kdf = tab(idx + 1.0), tab(c - 1.0 - idx)
    qdb, kdb = tab(c - idx), tab(idx)
    cdec = np.broadcast_to((gam ** c)[:, None, None], (RT_HEADS, 1, LANES))
    hp = RT_HEADS // 2
    f = lambda x: jnp.asarray(np.ascontiguousarray(x).reshape((hp, 2) + x.shape[1:]), F32)
    return f(dsym), f(mask[:, None, :]), f(qdf), f(kdf), f(qdb), f(kdb), f(cdec)


def _ret_kernel(need_ctx, chunk, *refs):
    (ql, kl, vl, gl, qc, kc, vc, gc, dsym_ref, msk_ref, qdf_ref, kdf_ref, qdb_ref, kdb_ref,
     cdec_ref) = refs[:15]
    rest = refs[15:]
    if need_ctx:
        yl_ref, yc_ref, of_ref, ob_ref, ocf_ref, ocb_ref, st_ref = rest
    else:
        yl_ref, of_ref, ob_ref, st_ref = rest
        yc_ref = ocf_ref = ocb_ref = None
    nc_lat = ql.shape[0] // chunk
    nc_ctx = qc.shape[0] // chunk
    st_ref[...] = jnp.zeros_like(st_ref)

    def step(j, q_ref, k_ref, v_ref, n, outf, outb):
        sf = pl.ds(pl.multiple_of(j * chunk, chunk), chunk)
        sb = pl.ds(pl.multiple_of((n - 1 - j) * chunk, chunk), chunk)
        qf = q_ref[sf, :].astype(F32)
        kfb = k_ref[sf, :]
        kf = kfb.astype(F32)
        qb = q_ref[sb, :].astype(F32)
        kb = k_ref[sb, :].astype(F32)
        for h in range(2):
            hs = slice(h * LANES, (h + 1) * LANES)
            vf = v_ref[sf, hs]
            vb = v_ref[sb, hs]
            a = lax.dot_general((qf * msk_ref[h]).astype(BF16), kfb, NT_DIMS,
                                preferred_element_type=F32) * dsym_ref[h]
            o = (jnp.dot(a.astype(BF16), vf, preferred_element_type=F32)
                 + jnp.dot((qf * qdf_ref[h]).astype(BF16), st_ref[h].astype(BF16),
                           preferred_element_type=F32))
            st_ref[h] = cdec_ref[h] * st_ref[h] + lax.dot_general(
                (kf * kdf_ref[h]).astype(BF16), vf, TN_DIMS, preferred_element_type=F32)
            if outf is not None:
                outf[sf, hs] = o
            o = jnp.dot((qb * qdb_ref[h]).astype(BF16), st_ref[2 + h].astype(BF16),
                        preferred_element_type=F32)
            st_ref[2 + h] = cdec_ref[h] * st_ref[2 + h] + lax.dot_general(
                (kb * kdb_ref[h]).astype(BF16), vb, TN_DIMS, preferred_element_type=F32)
            if outb is not None:
                outb[sb, hs] = o

    def ctx_body(j, carry):
        step(j, qc, kc, vc, nc_ctx, ocf_ref, ocb_ref)
        return carry

    def lat_body(j, carry):
        step(j, ql, kl, vl, nc_lat, of_ref, ob_ref)
        return carry

    lax.fori_loop(0, nc_ctx, ctx_body, 0)
    lax.fori_loop(0, nc_lat, lat_body, 0)

    def readout(n, a_ref, b_ref, g_ref, y_ref):
        def body(j, carry):
            s = pl.ds(pl.multiple_of(j * chunk, chunk), chunk)
            for h in range(2):
                hs = slice(h * LANES, (h + 1) * LANES)
                o = a_ref[s, hs] + b_ref[s, hs]
                o = o * lax.rsqrt(jnp.mean(o * o, axis=-1, keepdims=True) + NORM_EPS)
                y_ref[s, hs] = (o * _silu(g_ref[s, hs].astype(F32))).astype(y_ref.dtype)
            return carry
        lax.fori_loop(0, n, body, 0)

    readout(nc_lat, of_ref, ob_ref, gl, yl_ref)
    if need_ctx:
        readout(nc_ctx, ocf_ref, ocb_ref, gc, yc_ref)


def _retention(qr, kr, p, tables, batch, seq, ctx_len, need_ctx):
    n_lat = batch * seq
    chunk = _div_tile(math.gcd(seq, ctx_len), RT_CHUNK)
    ctx_blk0 = n_lat // ctx_len
    hp = RT_HEADS // 2
    pair = 2 * LANES
    width = RT_HEADS * LANES
    v_col, g_col = COL_RT_V * LANES // pair, COL_RT_G * LANES // pair

    in_specs = [
        pl.BlockSpec((seq, LANES), lambda b, h: (b, h)),
        pl.BlockSpec((seq, LANES), lambda b, h: (b, h)),
        pl.BlockSpec((seq, pair), lambda b, h: (b, v_col + h)),
        pl.BlockSpec((seq, pair), lambda b, h: (b, g_col + h)),
        pl.BlockSpec((ctx_len, LANES), lambda b, h: (ctx_blk0 + b, h)),
        pl.BlockSpec((ctx_len, LANES), lambda b, h: (ctx_blk0 + b, h)),
        pl.BlockSpec((ctx_len, pair), lambda b, h: (ctx_blk0 + b, v_col + h)),
        pl.BlockSpec((ctx_len, pair), lambda b, h: (ctx_blk0 + b, g_col + h)),
    ]
    for t in tables:
        in_specs.append(pl.BlockSpec((None,) + t.shape[1:], lambda b, h: (h, 0, 0, 0)))
    out_shape = [jax.ShapeDtypeStruct((n_lat, width), BF16)]
    out_specs = [pl.BlockSpec((seq, pair), lambda b, h: (b, h))]
    scratch = [pltpu.VMEM((seq, pair), F32), pltpu.VMEM((seq, pair), F32)]
    if need_ctx:
        out_shape.append(jax.ShapeDtypeStruct((batch * ctx_len, width), BF16))
        out_specs.append(pl.BlockSpec((ctx_len, pair), lambda b, h: (b, h)))
        scratch += [pltpu.VMEM((ctx_len, pair), F32), pltpu.VMEM((ctx_len, pair), F32)]
    scratch.append(pltpu.VMEM((4, LANES, LANES), F32))
    outs = pl.pallas_call(
        functools.partial(_ret_kernel, need_ctx, chunk),
        out_shape=tuple(out_shape),
        grid=(batch, hp),
        in_specs=in_specs,
        out_specs=tuple(out_specs),
        scratch_shapes=scratch,
        compiler_params=_cparams("parallel", "parallel"),
        name="retention_scan",
    )(qr, kr, p, p, qr, kr, p, p, *tables)
    return outs[0], (outs[1] if need_ctx else None)


def _merge_kernel(h_ref, y0_ref, y1_ref, y2_ref, wg_ref, bg_ref, wb_ref, o_ref):
    h = h_ref[...]
    acc = None
    for i, y_ref in enumerate((y0_ref, y1_ref, y2_ref)):
        gate = _sigmoid(jnp.dot(h, wg_ref[i], preferred_element_type=F32) + bg_ref[i])
        t = gate * jnp.dot(y_ref[...], wb_ref[i], preferred_element_type=F32)
        acc = t if acc is None else acc + t
    o_ref[...] = acc.astype(o_ref.dtype)


def _merge(h, ys, w_mgate, b_mgate, w_branch, rows):
    d = h.shape[1]
    tm = _div_tile(rows, MM_TM)
    tn = _div_tile(d, 256)
    ysp = pl.BlockSpec((tm, BRANCH_W), lambda i, j: (i, 0))
    return pl.pallas_call(
        _merge_kernel,
        out_shape=jax.ShapeDtypeStruct((rows, d), BF16),
        grid=(rows // tm, d // tn),
        in_specs=[pl.BlockSpec((tm, d), lambda i, j: (i, 0)), ysp, ysp, ysp,
                  pl.BlockSpec((3, d, tn), lambda i, j: (0, 0, j)),
                  pl.BlockSpec((3, 1, tn), lambda i, j: (0, 0, j)),
                  pl.BlockSpec((3, BRANCH_W, tn), lambda i, j: (0, 0, j))],
        out_specs=pl.BlockSpec((tm, tn), lambda i, j: (i, j)),
        compiler_params=_cparams("parallel", "arbitrary"),
        name="branch_merge",
    )(h, *ys, w_mgate, b_mgate.reshape(3, 1, d), w_branch)


def _proj_res_kernel(has_extra, *refs):
    if has_extra:
        a_ref, w_ref, x_ref, g_ref, e_ref, o_ref = refs
    else:
        a_ref, w_ref, x_ref, g_ref, o_ref = refs
    y = jnp.dot(a_ref[...], w_ref[...], preferred_element_type=F32)
    if has_extra:
        y = y + e_ref[...]
    o_ref[...] = x_ref[...] + g_ref[...] * y


def _proj_residual(a, w, x, gate, extra, rows, seq, n_lat, batch):
    k = a.shape[1]
    d = w.shape[1]
    tm = _div_tile(math.gcd(seq, rows), MM_TM)
    tn = _div_tile(d, 512)
    rmap = _mod_row_map(tm, seq, n_lat, batch)
    in_specs = [pl.BlockSpec((tm, k), lambda i, j: (i, 0)),
                pl.BlockSpec((k, tn), lambda i, j: (0, j)),
                pl.BlockSpec((tm, tn), lambda i, j: (i, j)),
                pl.BlockSpec((None, 1, tn), lambda i, j: rmap(i)[:2] + (j,))]
    args = [a, w, x, gate]
    if extra is not None:
        in_specs.append(pl.BlockSpec((tm, tn), lambda i, j: (i, j)))
        args.append(extra)
    return pl.pallas_call(
        functools.partial(_proj_res_kernel, extra is not None),
        out_shape=jax.ShapeDtypeStruct((rows, d), F32),
        grid=(rows // tm, d // tn),
        in_specs=in_specs,
        out_specs=pl.BlockSpec((tm, tn), lambda i, j: (i, j)),
        compiler_params=_cparams("parallel", "arbitrary"),
        name="proj_residual",
    )(*args)


def _router_kernel(x_ref, w_ref, sc_ref, sh_ref, wr_ref, rb_ref, h_ref, idx_ref, gate_ref):
    x = x_ref[...]
    ms = jnp.mean(x * x, axis=-1, keepdims=True)
    h = x * lax.rsqrt(ms + NORM_EPS) * w_ref[...] * (1.0 + sc_ref[...]) + sh_ref[...]
    h_ref[...] = h.astype(h_ref.dtype)
    tm = x.shape[0]
    logits = lax.dot_general(wr_ref[...], h, NT_DIMS, precision=lax.Precision.HIGHEST,
                             preferred_element_type=F32)
    scores = _sigmoid(logits)
    choice = scores + rb_ref[...]
    per_group = N_EXPERTS // N_GROUPS
    neg = -jnp.inf
    sub = lax.broadcasted_iota(jnp.int32, (per_group, tm), 0).astype(F32)
    gs = []
    for g in range(N_GROUPS):
        xg = choice[g * per_group:(g + 1) * per_group, :]
        m1 = jnp.max(xg, axis=0, keepdims=True)
        i1 = jnp.min(jnp.where(xg == m1, sub, float(per_group)), axis=0, keepdims=True)
        m2 = jnp.max(jnp.where(sub == i1, neg, xg), axis=0, keepdims=True)
        gs.append(m1 + m2)
    sel = [jnp.zeros((1, tm), F32) for _ in range(N_GROUPS)]
    for _ in range(TOPK_GROUPS):
        m = functools.reduce(jnp.maximum, gs)
        found = jnp.zeros((1, tm), F32)
        for g in range(N_GROUPS):
            hit = jnp.where(gs[g] == m, 1.0 - found, 0.0)
            sel[g] = sel[g] + hit
            found = found + hit
            gs[g] = jnp.where(hit > 0.5, neg, gs[g])
    masked = jnp.concatenate(
        [jnp.where(sel[g] > 0.5, choice[g * per_group:(g + 1) * per_group, :], neg)
         for g in range(N_GROUPS)], axis=0)
    eidx = lax.broadcasted_iota(jnp.int32, (N_EXPERTS, tm), 0).astype(F32)
    idxs, ws = [], []
    for _ in range(TOP_K):
        m = jnp.max(masked, axis=0, keepdims=True)
        i = jnp.min(jnp.where(masked == m, eidx, float(N_EXPERTS)), axis=0, keepdims=True)
        hit = eidx == i
        ws.append(jnp.sum(jnp.where(hit, scores, 0.0), axis=0, keepdims=True))
        idxs.append(i)
        masked = jnp.where(hit, neg, masked)
    wsum = functools.reduce(lambda a, b: a + b, ws)
    idx_ref[...] = jnp.concatenate(idxs, axis=0).astype(jnp.int32)
    gate_ref[...] = jnp.concatenate(ws, axis=0) / wsum * ROUTED_SCALE


def _norm_router(x, w, sc, sh, w_router, router_bias, rows, seq, n_lat, batch):
    d = x.shape[1]
    tm = _div_tile(math.gcd(seq, rows), ROW_TILE)
    rmap = _mod_row_map(tm, seq, n_lat, batch)
    return pl.pallas_call(
        _router_kernel,
        out_shape=(jax.ShapeDtypeStruct((rows, d), BF16),
                   jax.ShapeDtypeStruct((TOP_K, rows), jnp.int32),
                   jax.ShapeDtypeStruct((TOP_K, rows), F32)),
        grid=(rows // tm,),
        in_specs=[pl.BlockSpec((tm, d), lambda i: (i, 0)),
                  pl.BlockSpec((1, d), lambda i: (0, 0)),
                  pl.BlockSpec((None, 1, d), rmap),
                  pl.BlockSpec((None, 1, d), rmap),
                  pl.BlockSpec((N_EXPERTS, d), lambda i: (0, 0)),
                  pl.BlockSpec((N_EXPERTS, 1), lambda i: (0, 0))],
        out_specs=(pl.BlockSpec((tm, d), lambda i: (i, 0)),
                   pl.BlockSpec((TOP_K, tm), lambda i: (0, i)),
                   pl.BlockSpec((TOP_K, tm), lambda i: (0, i))),
        compiler_params=_cparams("parallel"),
        name="norm_router",
    )(x, w.reshape(1, d), sc, sh, w_router.T, router_bias.reshape(N_EXPERTS, 1))


def _expert_kernel(be_ref, nu_ref, x_ref, wgu_ref, wd_ref, wt_ref, o_ref):
    @pl.when(pl.program_id(0) < nu_ref[0])
    def _():
        gu = jnp.dot(x_ref[...], wgu_ref[...], preferred_element_type=F32)
        act = (_silu(gu[:, :EXPERT_FF]) * gu[:, EXPERT_FF:]).astype(BF16)
        y = jnp.dot(act, wd_ref[...], preferred_element_type=F32)
        o_ref[...] = (y * wt_ref[...]).astype(o_ref.dtype)

    @pl.when(pl.program_id(0) >= nu_ref[0])
    def _():
        o_ref[...] = jnp.zeros_like(o_ref)


def _experts(xs, wgu, wd, buf_w, block_expert, n_used, bm):
    rows, d = xs.shape
    nb = rows // bm
    grid_spec = pltpu.PrefetchScalarGridSpec(
        num_scalar_prefetch=2,
        grid=(nb,),
        in_specs=[pl.BlockSpec((bm, d), lambda i, be, nu: (i, 0)),
                  pl.BlockSpec((None, d, 2 * EXPERT_FF), lambda i, be, nu: (be[i], 0, 0)),
                  pl.BlockSpec((None, EXPERT_FF, d), lambda i, be, nu: (be[i], 0, 0)),
                  pl.BlockSpec((bm, 1), lambda i, be, nu: (i, 0))],
        out_specs=pl.BlockSpec((bm, d), lambda i, be, nu: (i, 0)),
    )
    return pl.pallas_call(
        _expert_kernel,
        out_shape=jax.ShapeDtypeStruct((rows, d), BF16),
        grid_spec=grid_spec,
        compiler_params=_cparams("arbitrary"),
        name="routed_experts",
    )(block_expert, n_used, xs, wgu, wd, buf_w.reshape(rows, 1))


def _dispatch_plan(idx, gates, bm):
    t = idx.shape[0]
    a = t * TOP_K
    flat_e = idx.reshape(a)
    order = jnp.argsort(flat_e, stable=True).astype(jnp.int32)
    e_sorted = flat_e[order]
    counts = jnp.zeros((N_EXPERTS,), jnp.int32).at[flat_e].add(1)
    starts = jnp.cumsum(counts) - counts
    padded = (counts + bm - 1) // bm * bm
    padded_end = jnp.cumsum(padded)
    dest = (padded_end - padded)[e_sorted] + jnp.arange(a, dtype=jnp.int32) - starts[e_sorted]
    nb = -(-a // bm) + N_EXPERTS
    size = nb * bm
    buf_tok = jnp.zeros((size,), jnp.int32).at[dest].set(order // TOP_K)
    buf_w = jnp.zeros((size,), F32).at[dest].set(gates.reshape(a)[order])
    pos = jnp.zeros((a,), jnp.int32).at[order].set(dest)
    block_start = jnp.arange(nb, dtype=jnp.int32) * bm
    block_expert = jnp.minimum(jnp.searchsorted(padded_end, block_start, side='right'),
                               N_EXPERTS - 1).astype(jnp.int32)
    n_used = (padded_end[-1:] // bm).astype(jnp.int32)
    return buf_tok, buf_w, pos, block_expert, n_used


def _swiglu_up_kernel(a_ref, wg_ref, wu_ref, o_ref):
    a = a_ref[...]
    g = jnp.dot(a, wg_ref[...], preferred_element_type=F32)
    u = jnp.dot(a, wu_ref[...], preferred_element_type=F32)
    o_ref[...] = (_silu(g) * u).astype(o_ref.dtype)


def _swiglu_up(a, wg, wu):
    m, k = a.shape
    n = wg.shape[1]
    tm = _div_tile(m, MM_TM)
    tn = _div_tile(n, 512)
    wsp = pl.BlockSpec((k, tn), lambda i, j: (0, j))
    return pl.pallas_call(
        _swiglu_up_kernel,
        out_shape=jax.ShapeDtypeStruct((m, n), BF16),
        grid=(m // tm, n // tn),
        in_specs=[pl.BlockSpec((tm, k), lambda i, j: (i, 0)), wsp, wsp],
        out_specs=pl.BlockSpec((tm, tn), lambda i, j: (i, j)),
        compiler_params=_cparams("parallel", "arbitrary"),
        name="shared_swiglu_up",
    )(a, wg, wu)


def _lambda_init_for(layer):
    return 0.8 - 0.6 * math.exp(-0.3 * layer)


def kernel(x, c, ctx, c_ctx, w_ada, b_ada, norm_mix, norm_ffn, w_in, hgrn_lb, hgrn_norm, diff_lambda, diff_subln, w_branch, w_mgate, b_mgate, w_out, w_router, router_bias, w_exp_gate, w_exp_up, w_exp_down, w_sh_gate, w_sh_up, w_sh_down, norm_final):
    batch, seq, d = x.shape
    ctx_len = ctx.shape[1]
    depth = w_ada.shape[0]
    n_lat, n_ctx = batch * seq, batch * ctx_len
    n_all = n_lat + n_ctx
    assert batch < 8 and seq % GRID_W == 0 and ctx_len % 8 == 0

    p_lb = jax.nn.softmax(hgrn_lb.astype(F32), axis=1)
    lower_bounds = jnp.cumsum(p_lb, axis=1) - p_lb[:, :1]

    c8 = jnp.zeros((8, d), F32).at[:batch].set(c).at[batch].set(c_ctx)
    mod = _ada_mod(c8, w_ada, b_ada)

    rope_tm = _div_tile(math.gcd(seq, n_ctx), ROW_TILE)
    rope_tabs = _rope_tables(seq, rope_tm)
    hg_tabs = _hgrn_tables(_div_tile(math.gcd(seq, ctx_len), HG_CHUNK))
    rt_tabs = _ret_tables(_div_tile(math.gcd(seq, ctx_len), RT_CHUNK))

    xa = jnp.concatenate([x.reshape(n_lat, d), ctx.reshape(n_ctx, d)], axis=0)
    for l in range(depth):
        need_ctx = l < depth - 1
        rows = n_all if need_ctx else n_lat
        sh1, sc1, g1, sh2, sc2, g2 = [mod[l, :, i * d:(i + 1) * d].reshape(8, 1, d) for i in range(6)]

        h = _norm_mod(xa, norm_mix[l], sc1, sh1, n_all, seq, n_lat, batch, BF16)
        p = _matmul(h, w_in[l].astype(BF16), BF16)
        qd, kd, qr, kr = _rope_prep(p, rope_tabs, rope_tm, seq, n_lat)
        y_hg, yc_hg = _hgrn(p, lower_bounds[0, l], lower_bounds[1, l], hgrn_norm[l], hg_tabs,
                            batch, seq, ctx_len, need_ctx)
        lam_init = _lambda_init_for(l)
        y_da = _diff_attn(qd, kd, p, diff_lambda[l], diff_subln[l], lam_init, batch, seq, ctx_len, True)
        y_rt, yc_rt = _retention(qr, kr, p, rt_tabs, batch, seq, ctx_len, need_ctx)
        ys = [y_hg, y_da, y_rt]
        if need_ctx:
            yc_da = _diff_attn(qd, kd, p, diff_lambda[l], diff_subln[l], lam_init, batch, seq,
                               ctx_len, False)
            ys = [jnp.concatenate([a, b], axis=0) for a, b in zip(ys, (yc_hg, yc_da, yc_rt))]
        merged = _merge(h, ys, w_mgate[l].astype(BF16), b_mgate[l], w_branch[l].astype(BF16), rows)
        xa = _proj_residual(merged, w_out[l].astype(BF16), xa, g1, None, rows, seq, n_lat, batch)

        h2, idx_t, gate_t = _norm_router(xa, norm_ffn[l], sc2, sh2, w_router[l], router_bias[l],
                                         rows, seq, n_lat, batch)
        buf_tok, buf_w, pos, block_expert, n_used = _dispatch_plan(idx_t.T, gate_t.T, MOE_BM)
        xs = jnp.take(h2, buf_tok, axis=0)
        wgu = jnp.concatenate([w_exp_gate[l], w_exp_up[l]], axis=-1).astype(BF16)
        y_sorted = _experts(xs, wgu, w_exp_down[l].astype(BF16), buf_w, block_expert, n_used, MOE_BM)
        y_routed = jnp.take(y_sorted, pos, axis=0).reshape(rows, TOP_K, d).astype(F32).sum(axis=1)
        up = _swiglu_up(h2, w_sh_gate[l].astype(BF16), w_sh_up[l].astype(BF16))
        xa = _proj_residual(up, w_sh_down[l].astype(BF16), xa, g2, y_routed, rows, seq, n_lat, batch)

    zeros = jnp.zeros((8, 1, d), F32)
    out = _norm_mod(xa, norm_final, zeros, zeros, n_lat, seq, n_lat, batch, F32)
    return out.reshape(batch, seq, d)
```

```python
import functools
import math

import numpy as np
import jax
import jax.numpy as jnp
from jax import lax
from jax.experimental import pallas as pl
from jax.experimental.pallas import tpu as pltpu

F32 = jnp.float32
BF16 = jnp.bfloat16

NORM_EPS = 1e-6
ROPE_BASE = 10000.0
GRID_W = 64
HG_HEADS = 8
DA_HEADS = 8
DA_DH = 64
RT_HEADS = 8
RT_DK = 64
N_EXPERTS = 64
TOP_K = 8
N_GROUPS = 8
TOPK_GROUPS = 4
EXPERT_FF = 256
ROUTED_SCALE = 2.5
BRANCH_W = 1024

LANES = 128
V7X_VMEM_BYTES = 64 * 1024 * 1024
VMEM_LIMIT = 56 * 1024 * 1024

COL_HG_Q, COL_HG_FF, COL_HG_FB, COL_HG_I, COL_HG_G = 0, 8, 16, 24, 32
COL_DA_Q, COL_DA_K, COL_DA_V = 40, 48, 56
COL_RT_Q, COL_RT_K, COL_RT_V, COL_RT_G = 64, 68, 72, 80

NT_DIMS = (((1,), (1,)), ((), ()))
TN_DIMS = (((0,), (0,)), ((), ()))

ROW_TILE = 256
MM_TM = 1024
HG_CHUNK = 128
RT_CHUNK = 256
ATT_TQ = 256
ATT_TK = 1024
ATT_VROWS = LANES + 16
MOE_BM = 256


def _cparams(*sem):
    return pltpu.CompilerParams(dimension_semantics=sem, vmem_limit_bytes=VMEM_LIMIT)


def _div_tile(n, pref):
    t = min(n, pref)
    while n % t:
        t //= 2
    return t


def _sigmoid(x):
    return 1.0 / (1.0 + jnp.exp(-x))


def _silu(x):
    return x * _sigmoid(x)


def _ada_kernel(c_ref, w_ref, b_ref, o_ref):
    a = _silu(c_ref[...]).astype(BF16)
    o_ref[...] = jnp.dot(a, w_ref[...].astype(BF16), preferred_element_type=F32) + b_ref[...]


def _ada_mod(c8, w_ada, b_ada):
    depth, d, n = w_ada.shape
    tn = _div_tile(n, 512)
    return pl.pallas_call(
        _ada_kernel,
        out_shape=jax.ShapeDtypeStruct((depth, 8, n), F32),
        grid=(depth, n // tn),
        in_specs=[
            pl.BlockSpec((8, d), lambda l, j: (0, 0)),
            pl.BlockSpec((None, d, tn), lambda l, j: (l, 0, j)),
            pl.BlockSpec((None, 1, tn), lambda l, j: (l, 0, j)),
        ],
        out_specs=pl.BlockSpec((None, 8, tn), lambda l, j: (l, 0, j)),
        compiler_params=_cparams("arbitrary", "arbitrary"),
        name="ada_mod",
    )(c8, w_ada, b_ada.reshape(depth, 1, n))


def _mod_row_map(tm, seq, n_lat, batch):
    def index_map(i):
        start = i * tm
        return (jnp.where(start < n_lat, start // seq, batch), 0, 0)
    return index_map


def _norm_mod_kernel(x_ref, w_ref, sc_ref, sh_ref, o_ref):
    x = x_ref[...]
    ms = jnp.mean(x * x, axis=-1, keepdims=True)
    y = x * lax.rsqrt(ms + NORM_EPS) * w_ref[...]
    o_ref[...] = (y * (1.0 + sc_ref[...]) + sh_ref[...]).astype(o_ref.dtype)


def _norm_mod(x, w, sc, sh, rows, seq, n_lat, batch, out_dtype):
    d = x.shape[1]
    tm = _div_tile(math.gcd(seq, rows), ROW_TILE)
    rmap = _mod_row_map(tm, seq, n_lat, batch)
    return pl.pallas_call(
        _norm_mod_kernel,
        out_shape=jax.ShapeDtypeStruct((rows, d), out_dtype),
        grid=(rows // tm,),
        in_specs=[
            pl.BlockSpec((tm, d), lambda i: (i, 0)),
            pl.BlockSpec((1, d), lambda i: (0, 0)),
            pl.BlockSpec((None, 1, d), rmap),
            pl.BlockSpec((None, 1, d), rmap),
        ],
        out_specs=pl.BlockSpec((tm, d), lambda i: (i, 0)),
        compiler_params=_cparams("parallel"),
        name="norm_mod",
    )(x, w.reshape(1, d), sc, sh)


def _mm_kernel(a_ref, b_ref, o_ref):
    o_ref[...] = jnp.dot(a_ref[...], b_ref[...], preferred_element_type=F32).astype(o_ref.dtype)


def _matmul(a, b, layer, out_dtype, tn_pref=1024):
    m, k = a.shape
    n = b.shape[2]
    tm = _div_tile(m, MM_TM)
    tn = _div_tile(n, tn_pref)
    return pl.pallas_call(
        _mm_kernel,
        out_shape=jax.ShapeDtypeStruct((m, n), out_dtype),
        grid=(m // tm, n // tn),
        in_specs=[pl.BlockSpec((tm, k), lambda i, j: (i, 0)),
                  pl.BlockSpec((None, k, tn), lambda i, j: (layer, 0, j))],
        out_specs=pl.BlockSpec((tm, tn), lambda i, j: (i, j)),
        compiler_params=_cparams("parallel", "arbitrary"),
        name="matmul",
    )(a, b)


def _rope_tables(seq, n_ctx):
    pos = np.arange(seq, dtype=np.float64)
    rows = np.floor(pos / GRID_W)
    cols = pos - rows * GRID_W
    lane = np.arange(LANES)
    j = lane % 64
    part, jj = j // 32, j % 32
    inv = ROPE_BASE ** (-np.arange(0, 32, 2, dtype=np.float64) / 32.0)
    ang = np.where(part[None, :] == 0, rows[:, None], cols[:, None]) * inv[jj % 16][None, :]
    cos_da = np.cos(ang)
    sin_da = np.sin(ang) * np.where(jj < 16, -1.0, 1.0)[None, :]
    partner_da = np.where(jj < 16, lane + 16, lane - 16)
    theta = 1.0 / (ROPE_BASE ** np.linspace(0.0, 1.0, 32))
    ang = pos[:, None] * theta[j % 32][None, :]
    cos_rt = np.cos(ang)
    sin_rt = np.sin(ang) * np.where(j < 32, -1.0, 1.0)[None, :]
    partner_rt = np.where(j < 32, lane + 32, lane - 32)

    def full(tab, fill):
        return jnp.asarray(np.concatenate([tab, np.full((n_ctx, LANES), fill)], 0), F32)

    def perm(partner):
        p = np.zeros((LANES, LANES), np.float32)
        p[partner, lane] = 1.0
        return jnp.asarray(p, BF16)

    return (full(cos_da, 1.0), full(sin_da, 0.0), full(cos_rt, 1.0), full(sin_rt, 0.0),
            perm(partner_da), perm(partner_rt))


def _rope_kernel(qd_ref, kd_ref, qr_ref, kr_ref, cd_ref, sd_ref, cr_ref, sr_ref, pd_ref, pr_ref,
                 oqd_ref, okd_ref, oqr_ref, okr_ref):
    def rot(src, dst, cos_ref, sin_ref, perm_ref, scale):
        cos = cos_ref[...]
        sin = sin_ref[...]
        perm = perm_ref[...]
        for g in range(src.shape[1] // LANES):
            sl = slice(g * LANES, (g + 1) * LANES)
            x = src[:, sl]
            xs = jnp.dot(x, perm, preferred_element_type=F32)
            y = x.astype(F32) * cos + xs * sin
            dst[:, sl] = (y * scale).astype(dst.dtype)

    rot(qd_ref, oqd_ref, cd_ref, sd_ref, pd_ref, DA_DH ** -0.5 * math.log2(math.e))
    rot(kd_ref, okd_ref, cd_ref, sd_ref, pd_ref, 1.0)
    rot(qr_ref, oqr_ref, cr_ref, sr_ref, pr_ref, 1.0)
    rot(kr_ref, okr_ref, cr_ref, sr_ref, pr_ref, RT_DK ** -0.5)


def _rope_prep(p, tables, tm, seq, n_lat):
    rows = p.shape[0]
    cos_da, sin_da, cos_rt, sin_rt, perm_da, perm_rt = tables
    lat_tiles, seq_tiles = n_lat // tm, seq // tm

    def tmap(i):
        return (jnp.where(i < lat_tiles, i % seq_tiles, seq_tiles), 0)

    wd, wr = DA_HEADS * 2 * DA_DH, RT_HEADS * RT_DK
    tab = pl.BlockSpec((tm, LANES), tmap)
    pm = pl.BlockSpec((LANES, LANES), lambda i: (0, 0))
    return pl.pallas_call(
        _rope_kernel,
        out_shape=(jax.ShapeDtypeStruct((rows, wd), BF16), jax.ShapeDtypeStruct((rows, wd), BF16),
                   jax.ShapeDtypeStruct((rows, wr), BF16), jax.ShapeDtypeStruct((rows, wr), BF16)),
        grid=(rows // tm,),
        in_specs=[
            pl.BlockSpec((tm, wd), lambda i: (i, COL_DA_Q * LANES // wd)),
            pl.BlockSpec((tm, wd), lambda i: (i, COL_DA_K * LANES // wd)),
            pl.BlockSpec((tm, wr), lambda i: (i, COL_RT_Q * LANES // wr)),
            pl.BlockSpec((tm, wr), lambda i: (i, COL_RT_K * LANES // wr)),
            tab, tab, tab, tab, pm, pm,
        ],
        out_specs=(pl.BlockSpec((tm, wd), lambda i: (i, 0)), pl.BlockSpec((tm, wd), lambda i: (i, 0)),
                   pl.BlockSpec((tm, wr), lambda i: (i, 0)), pl.BlockSpec((tm, wr), lambda i: (i, 0))),
        compiler_params=_cparams("parallel"),
        name="rope_prep",
    )(p, p, p, p, cos_da, sin_da, cos_rt, sin_rt, perm_da, perm_rt)


def _hgrn_tables(c):
    nlev = int(math.log2(c))
    m = np.zeros((nlev + 2, c, c), np.float32)
    w = np.zeros((nlev + 1, c, c), np.float32)
    for l in range(nlev):
        bs = c >> (l + 1)
        for t in range(c):
            blk = t // bs
            if blk % 2 == 1:
                m[l, t, blk * bs:t + 1] = 1.0
                w[l, t, (blk - 1) * bs:blk * bs] = 1.0
            else:
                m[l, t, t + 1:(blk + 1) * bs] = 1.0
    for t in range(c):
        m[nlev, t, :t + 1] = 1.0
        m[nlev + 1, t, t + 1:] = 1.0
    w[nlev] = np.eye(c)
    mb = m[:, ::-1, ::-1]
    wb = w[:, ::-1, ::-1]
    r = (nlev + 2) * c
    return (jnp.asarray(m.reshape(r, c), BF16), jnp.asarray(mb.reshape(r, c), BF16),
            jnp.asarray(w, F32), jnp.asarray(wb, F32))


def _hgrn_chunk(q_raw, f_raw, v, lb, m_ref, w_ref, st, fwd):
    c = q_raw.shape[0]
    nlev = w_ref.shape[0] - 1
    q = _silu(q_raw.astype(F32))
    f = lb + (1.0 - lb) * _sigmoid(f_raw.astype(F32))
    lf = jnp.log(f)
    k = 1.0 - f
    hi = lf.astype(BF16)
    lo = (lf - hi.astype(F32)).astype(BF16)
    g2 = jnp.dot(m_ref[...], jnp.concatenate([hi, lo], axis=1), preferred_element_type=F32)
    g = g2[:, :LANES] + g2[:, LANES:]
    e = jnp.exp(g)
    row = lax.broadcasted_iota(jnp.int32, (c, LANES), 0)
    a = w_ref[nlev] * lax.dot_general(q.astype(BF16), k.astype(BF16), NT_DIMS,
                                      preferred_element_type=F32)
    for l in range(nlev):
        shift = int(math.log2(c >> (l + 1)))
        odd = ((row >> shift) & 1) == 1
        x = (jnp.where(odd, q, k) if fwd else jnp.where(odd, k, q)) * e[l * c:(l + 1) * c]
        xb = x.astype(BF16)
        a = a + w_ref[l] * lax.dot_general(xb, xb, NT_DIMS, preferred_element_type=F32)
    qe = (q * e[nlev * c:(nlev + 1) * c]).astype(BF16)
    ke = (k * e[(nlev + 1) * c:(nlev + 2) * c]).astype(BF16)
    o = (jnp.dot(a.astype(BF16), v, preferred_element_type=F32)
         + lax.dot_general(qe, st.astype(BF16), NT_DIMS, preferred_element_type=F32))
    last = nlev * c + (c - 1 if fwd else 0)
    dec = jnp.exp(g[last:last + 1, :])
    st_new = dec * st + lax.dot_general(v, ke, TN_DIMS, preferred_element_type=F32)
    return o, st_new


def _hgrn_kernel(need_ctx, chunk, *refs):
    (ql, ffl, fbl, il, gl, qc, ffc, fbc, ic, gc, lbf_ref, lbb_ref, nw_ref,
     mf_ref, mb_ref, wf_ref, wb_ref) = refs[:17]
    rest = refs[17:]
    if need_ctx:
        yl_ref, yc_ref, of_ref, ob_ref, ocf_ref, ocb_ref, st_ref = rest
    else:
        yl_ref, of_ref, ob_ref, st_ref = rest
        yc_ref = ocf_ref = ocb_ref = None
    nc_lat = ql.shape[0] // chunk
    nc_ctx = qc.shape[0] // chunk
    lbf = lbf_ref[...]
    lbb = lbb_ref[...]
    st_ref[...] = jnp.zeros_like(st_ref)

    def step(j, q_ref, ff_ref, fb_ref, i_ref, n, outf, outb):
        sf = pl.ds(pl.multiple_of(j * chunk, chunk), chunk)
        sb = pl.ds(pl.multiple_of((n - 1 - j) * chunk, chunk), chunk)
        o, s = _hgrn_chunk(q_ref[sf, :], ff_ref[sf, :], i_ref[sf, :], lbf, mf_ref, wf_ref,
                           st_ref[0], True)
        st_ref[0] = s
        if outf is not None:
            outf[sf, :] = o
        o, s = _hgrn_chunk(q_ref[sb, :], fb_ref[sb, :], i_ref[sb, :], lbb, mb_ref, wb_ref,
                           st_ref[1], False)
        st_ref[1] = s
        if outb is not None:
            outb[sb, :] = o

    def ctx_body(j, carry):
        step(j, qc, ffc, fbc, ic, nc_ctx, ocf_ref, ocb_ref)
        return carry

    def lat_body(j, carry):
        step(j, ql, ffl, fbl, il, nc_lat, of_ref, ob_ref)
        return carry

    lax.fori_loop(0, nc_ctx, ctx_body, 0)
    lax.fori_loop(0, nc_lat, lat_body, 0, unroll=2)

    nw = nw_ref[...]

    def readout(n, a_ref, b_ref, g_ref, y_ref):
        def body(j, carry):
            s = pl.ds(pl.multiple_of(j * chunk, chunk), chunk)
            o = a_ref[s, :] + b_ref[s, :]
            o = o * lax.rsqrt(jnp.mean(o * o, axis=-1, keepdims=True) + NORM_EPS) * nw
            y_ref[s, :] = (o * _silu(g_ref[s, :].astype(F32))).astype(y_ref.dtype)
            return carry
        lax.fori_loop(0, n, body, 0)

    readout(nc_lat, of_ref, ob_ref, gl, yl_ref)
    if need_ctx:
        readout(nc_ctx, ocf_ref, ocb_ref, gc, yc_ref)


def _hgrn(p, lb_f, lb_b, norm_w, tables, batch, seq, ctx_len, need_ctx):
    n_lat = batch * seq
    chunk = _div_tile(math.gcd(seq, ctx_len), HG_CHUNK)
    mf, mb, wf, wb = tables
    ctx_blk0 = n_lat // ctx_len
    width = HG_HEADS * LANES

    def lat(col):
        return pl.BlockSpec((seq, LANES), lambda b, h: (b, col + h))

    def ctx(col):
        return pl.BlockSpec((ctx_len, LANES), lambda b, h: (ctx_blk0 + b, col + h))

    head_vec = pl.BlockSpec((1, LANES), lambda b, h: (0, h))
    const2 = lambda arr: pl.BlockSpec(arr.shape, lambda b, h: (0,) * arr.ndim)
    cols = (COL_HG_Q, COL_HG_FF, COL_HG_FB, COL_HG_I, COL_HG_G)
    in_specs = ([lat(c) for c in cols] + [ctx(c) for c in cols]
                + [head_vec, head_vec, pl.BlockSpec((1, LANES), lambda b, h: (0, 0)),
                   const2(mf), const2(mb), const2(wf), const2(wb)])
    out_shape = [jax.ShapeDtypeStruct((n_lat, width), BF16)]
    out_specs = [pl.BlockSpec((seq, LANES), lambda b, h: (b, h))]
    scratch = [pltpu.VMEM((seq, LANES), F32), pltpu.VMEM((seq, LANES), F32)]
    if need_ctx:
        out_shape.append(jax.ShapeDtypeStruct((batch * ctx_len, width), BF16))
        out_specs.append(pl.BlockSpec((ctx_len, LANES), lambda b, h: (b, h)))
        scratch += [pltpu.VMEM((ctx_len, LANES), F32), pltpu.VMEM((ctx_len, LANES), F32)]
    scratch.append(pltpu.VMEM((2, LANES, LANES), F32))
    outs = pl.pallas_call(
        functools.partial(_hgrn_kernel, need_ctx, chunk),
        out_shape=tuple(out_shape),
        grid=(batch, HG_HEADS),
        in_specs=in_specs,
        out_specs=tuple(out_specs),
        scratch_shapes=scratch,
        compiler_params=_cparams("parallel", "parallel"),
        name="hgrn2_scan",
    )(*([p] * 10), lb_f.reshape(1, width), lb_b.reshape(1, width), norm_w.reshape(1, LANES),
      mf, mb, wf, wb)
    return outs[0], (outs[1] if need_ctx else None)


def _attn_kernel(has_lat, tk, lambda_init, *refs):
    if has_lat:
        (q_ref, kc_ref, vc_ref, kl_ref, vl_ref, lam_ref, sub_ref, o_ref,
         s_ref, vtc_ref, vtl_ref) = refs
    else:
        q_ref, kc_ref, vc_ref, lam_ref, sub_ref, o_ref, s_ref, vtc_ref = refs
    tq = q_ref.shape[0]
    w2 = 2 * tq
    nctx = kc_ref.shape[0]
    nlat = kl_ref.shape[0] if has_lat else 0
    n_chunks = nlat // tk
    sub_t = min(256, tk)

    @pl.when(pl.program_id(2) == 0)
    def _():
        vtc_ref[LANES:, :] = jnp.ones((ATT_VROWS - LANES, nctx), BF16)
        for r in range(0, nctx, min(sub_t, nctx)):
            rr = slice(r, r + min(sub_t, nctx))
            vtc_ref[:LANES, rr] = vc_ref[rr, :].astype(F32).T.astype(BF16)
        for j in range(n_chunks):
            vtl_ref[j, LANES:, :] = jnp.ones((ATT_VROWS - LANES, tk), BF16)
            for r in range(0, tk, sub_t):
                vtl_ref[j, :LANES, r:r + sub_t] = (
                    vl_ref[j * tk + r:j * tk + r + sub_t, :].astype(F32).T.astype(BF16))

    q = q_ref[...].astype(F32).T
    sub = lax.broadcasted_iota(jnp.int32, (LANES, tq), 0)
    qst = jnp.concatenate([jnp.where(sub < DA_DH, q, 0.0), jnp.where(sub >= DA_DH, q, 0.0)],
                          axis=1).astype(BF16)

    def col_fold(x, op):
        return op(x.reshape(x.shape[0] // 8, 8, w2), axis=0)

    def scores(k_ref, src, dst, macc):
        s = jnp.dot(k_ref[src, :], qst, preferred_element_type=F32)
        s_ref[dst, :] = s
        return jnp.maximum(macc, col_fold(s, jnp.max))

    macc = scores(kc_ref, slice(0, nctx), slice(0, nctx), jnp.full((8, w2), -jnp.inf, F32))
    for j in range(n_chunks):
        macc = scores(kl_ref, slice(j * tk, (j + 1) * tk), slice(nctx + j * tk, nctx + (j + 1) * tk),
                      macc)
    m = jnp.max(macc, axis=0, keepdims=True)

    def weights(rows):
        return jnp.exp2((s_ref[rows, :] - m).astype(BF16))

    pv = jnp.dot(vtc_ref[...], weights(slice(0, nctx)), preferred_element_type=F32)
    for j in range(n_chunks):
        pv = pv + jnp.dot(vtl_ref[j], weights(slice(nctx + j * tk, nctx + (j + 1) * tk)),
                          preferred_element_type=F32)
    on = pv[:LANES] / pv[LANES:LANES + 1]
    lv = lam_ref[...]
    lam = (jnp.exp(jnp.sum(lv[0:1] * lv[1:2], axis=-1, keepdims=True))
           - jnp.exp(jnp.sum(lv[2:3] * lv[3:4], axis=-1, keepdims=True)) + lambda_init)
    o = on[:, :tq] - lam * on[:, tq:]
    o = o * lax.rsqrt(jnp.mean(o * o, axis=0, keepdims=True) + NORM_EPS) * sub_ref[...]
    o_ref[...] = (o * (1.0 - lambda_init)).T.astype(o_ref.dtype)


def _diff_attn(qd, kd, p, lam_vec, subln_w, lambda_init, batch, seq, ctx_len, latent_queries):
    n_lat = batch * seq
    ctx_blk0 = n_lat // ctx_len
    width = DA_HEADS * LANES
    lq = seq if latent_queries else ctx_len
    tq = _div_tile(lq, ATT_TQ)
    q_blk0 = 0 if latent_queries else n_lat // tq
    qt = lq // tq
    tk = _div_tile(seq, ATT_TK)
    in_specs = [
        pl.BlockSpec((tq, LANES), lambda b, h, i: (q_blk0 + b * qt + i, h)),
        pl.BlockSpec((ctx_len, LANES), lambda b, h, i: (ctx_blk0 + b, h)),
        pl.BlockSpec((ctx_len, LANES), lambda b, h, i: (ctx_blk0 + b, COL_DA_V + h)),
    ]
    args = [qd, kd, p]
    n_keys = ctx_len + (seq if latent_queries else 0)
    scratch = [pltpu.VMEM((n_keys, 2 * tq), F32), pltpu.VMEM((ATT_VROWS, ctx_len), BF16)]
    if latent_queries:
        in_specs += [pl.BlockSpec((seq, LANES), lambda b, h, i: (b, h)),
                     pl.BlockSpec((seq, LANES), lambda b, h, i: (b, COL_DA_V + h))]
        args += [kd, p]
        scratch.append(pltpu.VMEM((seq // tk, ATT_VROWS, tk), BF16))
    in_specs += [pl.BlockSpec((4, DA_DH), lambda b, h, i: (0, 0)),
                 pl.BlockSpec((LANES, 1), lambda b, h, i: (0, 0))]
    args += [lam_vec, subln_w.reshape(LANES, 1)]
    return pl.pallas_call(
        functools.partial(_attn_kernel, latent_queries, tk, lambda_init),
        out_shape=jax.ShapeDtypeStruct((batch * lq, width), BF16),
        grid=(batch, DA_HEADS, qt),
        in_specs=in_specs,
        out_specs=pl.BlockSpec((tq, LANES), lambda b, h, i: (b * qt + i, h)),
        scratch_shapes=scratch,
        compiler_params=_cparams("parallel", "parallel", "arbitrary"),
        name="diff_attn_lat" if latent_queries else "diff_attn_ctx",
    )(*args)


def _ret_tables(c):
    gam = 1.0 - 2.0 ** (-5.0 - np.arange(RT_HEADS, dtype=np.float64))
    idx = np.arange(c, dtype=np.float64)
    dist = np.abs(idx[:, None] - idx[None, :])
    dsym = gam[:, None, None] ** dist[None] * np.where(dist == 0, 2.0, 1.0)[None]
    lane_head = np.arange(LANES) // RT_DK
    mask = (lane_head[None, :] == (np.arange(RT_HEADS) % 2)[:, None]).astype(np.float64)

    def tab(power):
        return (gam[:, None] ** power[None, :])[:, :, None] * mask[:, None, :]

    qdf, kdf = tab(idx + 1.0), tab(c - 1.0 - idx)
    qdb, kdb = tab(c - idx), tab(idx)
    cdec = np.broadcast_to((gam ** c)[:, None, None], (RT_HEADS, 1, LANES))
    hp = RT_HEADS // 2
    f = lambda x: jnp.asarray(np.ascontiguousarray(x).reshape((hp, 2) + x.shape[1:]), F32)
    return f(dsym), f(mask[:, None, :]), f(qdf), f(kdf), f(qdb), f(kdb), f(cdec)


def _ret_kernel(need_ctx, chunk, *refs):
    (ql, kl, vl, gl, qc, kc, vc, gc, dsym_ref, msk_ref, qdf_ref, kdf_ref, qdb_ref, kdb_ref,
     cdec_ref) = refs[:15]
    rest = refs[15:]
    if need_ctx:
        yl_ref, yc_ref, of_ref, ob_ref, ocf_ref, ocb_ref, st_ref = rest
    else:
        yl_ref, of_ref, ob_ref, st_ref = rest
        yc_ref = ocf_ref = ocb_ref = None
    nc_lat = ql.shape[0] // chunk
    nc_ctx = qc.shape[0] // chunk
    st_ref[...] = jnp.zeros_like(st_ref)

    def step(j, q_ref, k_ref, v_ref, n, outf, outb):
        sf = pl.ds(pl.multiple_of(j * chunk, chunk), chunk)
        sb = pl.ds(pl.multiple_of((n - 1 - j) * chunk, chunk), chunk)
        qf = q_ref[sf, :].astype(F32)
        kfb = k_ref[sf, :]
        kf = kfb.astype(F32)
        qb = q_ref[sb, :].astype(F32)
        kb = k_ref[sb, :].astype(F32)
        for h in range(2):
            hs = slice(h * LANES, (h + 1) * LANES)
            vf = v_ref[sf, hs]
            vb = v_ref[sb, hs]
            a = lax.dot_general((qf * msk_ref[h]).astype(BF16), kfb, NT_DIMS,
                                preferred_element_type=F32) * dsym_ref[h]
            o = (jnp.dot(a.astype(BF16), vf, preferred_element_type=F32)
                 + jnp.dot((qf * qdf_ref[h]).astype(BF16), st_ref[h].astype(BF16),
                           preferred_element_type=F32))
            st_ref[h] = cdec_ref[h] * st_ref[h] + lax.dot_general(
                (kf * kdf_ref[h]).astype(BF16), vf, TN_DIMS, preferred_element_type=F32)
            if outf is not None:
                outf[sf, hs] = o
            o = jnp.dot((qb * qdb_ref[h]).astype(BF16), st_ref[2 + h].astype(BF16),
                        preferred_element_type=F32)
            st_ref[2 + h] = cdec_ref[h] * st_ref[2 + h] + lax.dot_general(
                (kb * kdb_ref[h]).astype(BF16), vb, TN_DIMS, preferred_element_type=F32)
            if outb is not None:
                outb[sb, hs] = o

    def ctx_body(j, carry):
        step(j, qc, kc, vc, nc_ctx, ocf_ref, ocb_ref)
        return carry

    def lat_body(j, carry):
        step(j, ql, kl, vl, nc_lat, of_ref, ob_ref)
        return carry

    lax.fori_loop(0, nc_ctx, ctx_body, 0)
    lax.fori_loop(0, nc_lat, lat_body, 0)

    def readout(n, a_ref, b_ref, g_ref, y_ref):
        def body(j, carry):
            s = pl.ds(pl.multiple_of(j * chunk, chunk), chunk)
            for h in range(2):
                hs = slice(h * LANES, (h + 1) * LANES)
                o = a_ref[s, hs] + b_ref[s, hs]
                o = o * lax.rsqrt(jnp.mean(o * o, axis=-1, keepdims=True) + NORM_EPS)
                y_ref[s, hs] = (o * _silu(g_ref[s, hs].astype(F32))).astype(y_ref.dtype)
            return carry
        lax.fori_loop(0, n, body, 0)

    readout(nc_lat, of_ref, ob_ref, gl, yl_ref)
    if need_ctx:
        readout(nc_ctx, ocf_ref, ocb_ref, gc, yc_ref)


def _retention(qr, kr, p, tables, batch, seq, ctx_len, need_ctx):
    n_lat = batch * seq
    chunk = _div_tile(math.gcd(seq, ctx_len), RT_CHUNK)
    ctx_blk0 = n_lat // ctx_len
    hp = RT_HEADS // 2
    pair = 2 * LANES
    width = RT_HEADS * LANES
    v_col, g_col = COL_RT_V * LANES // pair, COL_RT_G * LANES // pair

    in_specs = [
        pl.BlockSpec((seq, LANES), lambda b, h: (b, h)),
        pl.BlockSpec((seq, LANES), lambda b, h: (b, h)),
        pl.BlockSpec((seq, pair), lambda b, h: (b, v_col + h)),
        pl.BlockSpec((seq, pair), lambda b, h: (b, g_col + h)),
        pl.BlockSpec((ctx_len, LANES), lambda b, h: (ctx_blk0 + b, h)),
        pl.BlockSpec((ctx_len, LANES), lambda b, h: (ctx_blk0 + b, h)),
        pl.BlockSpec((ctx_len, pair), lambda b, h: (ctx_blk0 + b, v_col + h)),
        pl.BlockSpec((ctx_len, pair), lambda b, h: (ctx_blk0 + b, g_col + h)),
    ]
    for t in tables:
        in_specs.append(pl.BlockSpec((None,) + t.shape[1:], lambda b, h: (h, 0, 0, 0)))
    out_shape = [jax.ShapeDtypeStruct((n_lat, width), BF16)]
    out_specs = [pl.BlockSpec((seq, pair), lambda b, h: (b, h))]
    scratch = [pltpu.VMEM((seq, pair), F32), pltpu.VMEM((seq, pair), F32)]
    if need_ctx:
        out_shape.append(jax.ShapeDtypeStruct((batch * ctx_len, width), BF16))
        out_specs.append(pl.BlockSpec((ctx_len, pair), lambda b, h: (b, h)))
        scratch += [pltpu.VMEM((ctx_len, pair), F32), pltpu.VMEM((ctx_len, pair), F32)]
    scratch.append(pltpu.VMEM((4, LANES, LANES), F32))
    outs = pl.pallas_call(
        functools.partial(_ret_kernel, need_ctx, chunk),
        out_shape=tuple(out_shape),
        grid=(batch, hp),
        in_specs=in_specs,
        out_specs=tuple(out_specs),
        scratch_shapes=scratch,
        compiler_params=_cparams("parallel", "parallel"),
        name="retention_scan",
    )(qr, kr, p, p, qr, kr, p, p, *tables)
    return outs[0], (outs[1] if need_ctx else None)


def _merge_kernel(h_ref, y0_ref, y1_ref, y2_ref, wg_ref, bg_ref, wb_ref, o_ref):
    h = h_ref[...]
    acc = None
    for i, y_ref in enumerate((y0_ref, y1_ref, y2_ref)):
        gate = _sigmoid(jnp.dot(h, wg_ref[i], preferred_element_type=F32) + bg_ref[i])
        t = gate * jnp.dot(y_ref[...], wb_ref[i], preferred_element_type=F32)
        acc = t if acc is None else acc + t
    o_ref[...] = acc.astype(o_ref.dtype)


def _merge(h, ys, w_mgate, b_mgate, w_branch, layer, rows):
    d = h.shape[1]
    tm = _div_tile(rows, MM_TM)
    tn = _div_tile(d, 256)
    ysp = pl.BlockSpec((tm, BRANCH_W), lambda i, j: (i, 0))
    return pl.pallas_call(
        _merge_kernel,
        out_shape=jax.ShapeDtypeStruct((rows, d), BF16),
        grid=(rows // tm, d // tn),
        in_specs=[pl.BlockSpec((tm, d), lambda i, j: (i, 0)), ysp, ysp, ysp,
                  pl.BlockSpec((None, 3, d, tn), lambda i, j: (layer, 0, 0, j)),
                  pl.BlockSpec((None, 3, 1, tn), lambda i, j: (layer, 0, 0, j)),
                  pl.BlockSpec((None, 3, BRANCH_W, tn), lambda i, j: (layer, 0, 0, j))],
        out_specs=pl.BlockSpec((tm, tn), lambda i, j: (i, j)),
        compiler_params=_cparams("parallel", "arbitrary"),
        name="branch_merge",
    )(h, *ys, w_mgate, b_mgate.reshape(b_mgate.shape[0], 3, 1, d), w_branch)


def _proj_res_kernel(has_extra, *refs):
    if has_extra:
        a_ref, w_ref, x_ref, g_ref, e_ref, o_ref = refs
    else:
        a_ref, w_ref, x_ref, g_ref, o_ref = refs
    y = jnp.dot(a_ref[...], w_ref[...], preferred_element_type=F32)
    if has_extra:
        for s in range(e_ref.shape[0]):
            y = y + e_ref[s].astype(F32)
    o_ref[...] = x_ref[...] + g_ref[...] * y


def _proj_residual(a, w, layer, x, gate, extra, rows, seq, n_lat, batch):
    k = a.shape[1]
    d = w.shape[2]
    tm = _div_tile(math.gcd(seq, rows), MM_TM)
    tn = _div_tile(d, 512)
    rmap = _mod_row_map(tm, seq, n_lat, batch)
    in_specs = [pl.BlockSpec((tm, k), lambda i, j: (i, 0)),
                pl.BlockSpec((None, k, tn), lambda i, j: (layer, 0, j)),
                pl.BlockSpec((tm, tn), lambda i, j: (i, j)),
                pl.BlockSpec((None, 1, tn), lambda i, j: rmap(i)[:2] + (j,))]
    args = [a, w, x, gate]
    if extra is not None:
        in_specs.append(pl.BlockSpec((extra.shape[0], tm, tn), lambda i, j: (0, i, j)))
        args.append(extra)
    return pl.pallas_call(
        functools.partial(_proj_res_kernel, extra is not None),
        out_shape=jax.ShapeDtypeStruct((rows, d), F32),
        grid=(rows // tm, d // tn),
        in_specs=in_specs,
        out_specs=pl.BlockSpec((tm, tn), lambda i, j: (i, j)),
        compiler_params=_cparams("parallel", "arbitrary"),
        name="proj_residual",
    )(*args)


def _router_kernel(x_ref, w_ref, sc_ref, sh_ref, wr_ref, rb_ref, h_ref, idx_ref, gate_ref):
    x = x_ref[...]
    ms = jnp.mean(x * x, axis=-1, keepdims=True)
    h = x * lax.rsqrt(ms + NORM_EPS) * w_ref[...] * (1.0 + sc_ref[...]) + sh_ref[...]
    h_ref[...] = h.astype(h_ref.dtype)
    tm = x.shape[0]
    logits = lax.dot_general(wr_ref[...], h, NT_DIMS, precision=lax.Precision.HIGHEST,
                             preferred_element_type=F32)
    scores = _sigmoid(logits)
    choice = scores + rb_ref[...]
    per_group = N_EXPERTS // N_GROUPS
    neg = -jnp.inf
    sub = lax.broadcasted_iota(jnp.int32, (per_group, tm), 0).astype(F32)
    gs = []
    for g in range(N_GROUPS):
        xg = choice[g * per_group:(g + 1) * per_group, :]
        m1 = jnp.max(xg, axis=0, keepdims=True)
        i1 = jnp.min(jnp.where(xg == m1, sub, float(per_group)), axis=0, keepdims=True)
        m2 = jnp.max(jnp.where(sub == i1, neg, xg), axis=0, keepdims=True)
        gs.append(m1 + m2)
    sel = [jnp.zeros((1, tm), F32) for _ in range(N_GROUPS)]
    for _ in range(TOPK_GROUPS):
        m = functools.reduce(jnp.maximum, gs)
        found = jnp.zeros((1, tm), F32)
        for g in range(N_GROUPS):
            hit = jnp.where(gs[g] == m, 1.0 - found, 0.0)
            sel[g] = sel[g] + hit
            found = found + hit
            gs[g] = jnp.where(hit > 0.5, neg, gs[g])
    masked = jnp.concatenate(
        [jnp.where(sel[g] > 0.5, choice[g * per_group:(g + 1) * per_group, :], neg)
         for g in range(N_GROUPS)], axis=0)
    eidx = lax.broadcasted_iota(jnp.int32, (N_EXPERTS, tm), 0).astype(F32)
    idxs, ws = [], []
    for _ in range(TOP_K):
        m = jnp.max(masked, axis=0, keepdims=True)
        i = jnp.min(jnp.where(masked == m, eidx, float(N_EXPERTS)), axis=0, keepdims=True)
        hit = eidx == i
        ws.append(jnp.sum(jnp.where(hit, scores, 0.0), axis=0, keepdims=True))
        idxs.append(i)
        masked = jnp.where(hit, neg, masked)
    wsum = functools.reduce(lambda a, b: a + b, ws)
    idx_ref[...] = jnp.concatenate(idxs, axis=0).astype(jnp.int32)
    gate_ref[...] = jnp.concatenate(ws, axis=0) / wsum * ROUTED_SCALE


def _norm_router(x, w, sc, sh, w_router, router_bias, rows, seq, n_lat, batch):
    d = x.shape[1]
    tm = _div_tile(math.gcd(seq, rows), ROW_TILE)
    rmap = _mod_row_map(tm, seq, n_lat, batch)
    return pl.pallas_call(
        _router_kernel,
        out_shape=(jax.ShapeDtypeStruct((rows, d), BF16),
                   jax.ShapeDtypeStruct((TOP_K, rows), jnp.int32),
                   jax.ShapeDtypeStruct((TOP_K, rows), F32)),
        grid=(rows // tm,),
        in_specs=[pl.BlockSpec((tm, d), lambda i: (i, 0)),
                  pl.BlockSpec((1, d), lambda i: (0, 0)),
                  pl.BlockSpec((None, 1, d), rmap),
                  pl.BlockSpec((None, 1, d), rmap),
                  pl.BlockSpec((N_EXPERTS, d), lambda i: (0, 0)),
                  pl.BlockSpec((N_EXPERTS, 1), lambda i: (0, 0))],
        out_specs=(pl.BlockSpec((tm, d), lambda i: (i, 0)),
                   pl.BlockSpec((TOP_K, tm), lambda i: (0, i)),
                   pl.BlockSpec((TOP_K, tm), lambda i: (0, i))),
        compiler_params=_cparams("parallel"),
        name="norm_router",
    )(x, w.reshape(1, d), sc, sh, w_router.T, router_bias.reshape(N_EXPERTS, 1))


def _expert_kernel(be_ref, nu_ref, x_ref, wg_ref, wu_ref, wd_ref, wt_ref, o_ref,
                   wgb_ref, wub_ref, wdb_ref):
    i = pl.program_id(0)

    @pl.when((i == 0) | (be_ref[i] != be_ref[jnp.maximum(i - 1, 0)]))
    def _():
        wgb_ref[...] = wg_ref[...].astype(BF16)
        wub_ref[...] = wu_ref[...].astype(BF16)
        wdb_ref[...] = wd_ref[...].astype(BF16)

    @pl.when(i < nu_ref[0])
    def _():
        x = x_ref[...]
        g = jnp.dot(x, wgb_ref[...], preferred_element_type=F32)
        u = jnp.dot(x, wub_ref[...], preferred_element_type=F32)
        act = (_silu(g) * u).astype(BF16)
        y = jnp.dot(act, wdb_ref[...], preferred_element_type=F32)
        o_ref[...] = (y * wt_ref[...]).astype(o_ref.dtype)

    @pl.when(i >= nu_ref[0])
    def _():
        o_ref[...] = jnp.zeros_like(o_ref)


def _experts(xs, w_gate, w_up, w_down, layer, buf_w, block_expert, n_used, bm):
    rows, d = xs.shape
    nb = rows // bm
    ff = w_gate.shape[3]
    grid_spec = pltpu.PrefetchScalarGridSpec(
        num_scalar_prefetch=2,
        grid=(nb,),
        in_specs=[pl.BlockSpec((bm, d), lambda i, be, nu: (i, 0)),
                  pl.BlockSpec((None, None, d, ff), lambda i, be, nu: (layer, be[i], 0, 0)),
                  pl.BlockSpec((None, None, d, ff), lambda i, be, nu: (layer, be[i], 0, 0)),
                  pl.BlockSpec((None, None, ff, d), lambda i, be, nu: (layer, be[i], 0, 0)),
                  pl.BlockSpec((bm, 1), lambda i, be, nu: (i, 0))],
        out_specs=pl.BlockSpec((bm, d), lambda i, be, nu: (i, 0)),
        scratch_shapes=[pltpu.VMEM((d, ff), BF16), pltpu.VMEM((d, ff), BF16),
                        pltpu.VMEM((ff, d), BF16)],
    )
    return pl.pallas_call(
        _expert_kernel,
        out_shape=jax.ShapeDtypeStruct((rows, d), BF16),
        grid_spec=grid_spec,
        compiler_params=_cparams("arbitrary"),
        name="routed_experts",
    )(block_expert, n_used, xs, w_gate, w_up, w_down, buf_w.reshape(rows, 1))


def _take_rows(arr, idx):
    return arr.at[idx].get(mode="promise_in_bounds")


def _dispatch_plan(idx_t, gate_t, bm):
    k, t = idx_t.shape
    a = k * t
    experts = jnp.arange(N_EXPERTS, dtype=jnp.int32)
    flat_e = idx_t.reshape(a)
    iota = jnp.arange(a, dtype=jnp.int32)
    _, order = lax.sort_key_val(flat_e, iota)
    _, inv = lax.sort_key_val(order, iota)
    counts = jnp.sum((flat_e[:, None] == experts[None, :]).astype(jnp.int32), axis=0)
    starts = jnp.cumsum(counts) - counts
    padded = (counts + bm - 1) // bm * bm
    padded_end = jnp.cumsum(padded)
    padded_start = padded_end - padded
    nb = -(-a // bm) + N_EXPERTS
    n_used = (padded_end[-1:] // bm).astype(jnp.int32)
    blk = jnp.arange(nb, dtype=jnp.int32)
    block_expert = jnp.minimum(
        jnp.sum((blk[:, None] * bm >= padded_end[None, :]).astype(jnp.int32), axis=1), N_EXPERTS - 1)
    onehot = (block_expert[:, None] == experts[None, :]).astype(jnp.int32)
    b_pstart, b_count, b_start = [jnp.sum(onehot * v[None, :], axis=1)
                                  for v in (padded_start, counts, starts)]
    rank = blk[:, None] * bm + jnp.arange(bm, dtype=jnp.int32)[None, :] - b_pstart[:, None]
    valid = (rank < b_count[:, None]).reshape(nb * bm)
    src = jnp.clip(b_start[:, None] + rank, 0, a - 1).reshape(nb * bm)
    asg = _take_rows(order, src)
    buf_tok = jnp.where(valid, asg % t, 0)
    buf_w = jnp.where(valid, _take_rows(gate_t.reshape(a), asg), 0.0)
    shift = padded_start - starts
    pos = inv + _take_rows(shift, flat_e)
    return buf_tok, buf_w, pos.reshape(k, t), block_expert.astype(jnp.int32), n_used


def _swiglu_up_kernel(a_ref, wg_ref, wu_ref, o_ref):
    a = a_ref[...]
    g = jnp.dot(a, wg_ref[...], preferred_element_type=F32)
    u = jnp.dot(a, wu_ref[...], preferred_element_type=F32)
    o_ref[...] = (_silu(g) * u).astype(o_ref.dtype)


def _swiglu_up(a, wg, wu, layer):
    m, k = a.shape
    n = wg.shape[2]
    tm = _div_tile(m, MM_TM)
    tn = _div_tile(n, 512)
    wsp = pl.BlockSpec((None, k, tn), lambda i, j: (layer, 0, j))
    return pl.pallas_call(
        _swiglu_up_kernel,
        out_shape=jax.ShapeDtypeStruct((m, n), BF16),
        grid=(m // tm, n // tn),
        in_specs=[pl.BlockSpec((tm, k), lambda i, j: (i, 0)), wsp, wsp],
        out_specs=pl.BlockSpec((tm, tn), lambda i, j: (i, j)),
        compiler_params=_cparams("parallel", "arbitrary"),
        name="shared_swiglu_up",
    )(a, wg, wu)


def _lambda_init_for(layer):
    return 0.8 - 0.6 * math.exp(-0.3 * layer)


def kernel(x, c, ctx, c_ctx, w_ada, b_ada, norm_mix, norm_ffn, w_in, hgrn_lb, hgrn_norm, diff_lambda, diff_subln, w_branch, w_mgate, b_mgate, w_out, w_router, router_bias, w_exp_gate, w_exp_up, w_exp_down, w_sh_gate, w_sh_up, w_sh_down, norm_final):
    batch, seq, d = x.shape
    ctx_len = ctx.shape[1]
    depth = w_ada.shape[0]
    n_lat, n_ctx = batch * seq, batch * ctx_len
    n_all = n_lat + n_ctx
    assert batch < 8 and seq % GRID_W == 0 and ctx_len % 8 == 0

    p_lb = jax.nn.softmax(hgrn_lb.astype(F32), axis=1)
    lower_bounds = jnp.cumsum(p_lb, axis=1) - p_lb[:, :1]

    c8 = jnp.zeros((8, d), F32).at[:batch].set(c).at[batch].set(c_ctx)
    mod = _ada_mod(c8, w_ada, b_ada)

    rope_tm = _div_tile(math.gcd(seq, n_ctx), ROW_TILE)
    rope_tabs = _rope_tables(seq, rope_tm)
    hg_tabs = _hgrn_tables(_div_tile(math.gcd(seq, ctx_len), HG_CHUNK))
    rt_tabs = _ret_tables(_div_tile(math.gcd(seq, ctx_len), RT_CHUNK))

    w_in_b, w_mgate_b, w_branch_b, w_out_b = [w.astype(BF16) for w in (w_in, w_mgate, w_branch, w_out)]
    w_sg_b, w_su_b, w_sd_b = [w.astype(BF16) for w in (w_sh_gate, w_sh_up, w_sh_down)]

    xa = jnp.concatenate([x.reshape(n_lat, d), ctx.reshape(n_ctx, d)], axis=0)
    for l in range(depth):
        need_ctx = l < depth - 1
        rows = n_all if need_ctx else n_lat
        sh1, sc1, g1, sh2, sc2, g2 = [mod[l, :, i * d:(i + 1) * d].reshape(8, 1, d) for i in range(6)]

        h = _norm_mod(xa, norm_mix[l], sc1, sh1, n_all, seq, n_lat, batch, BF16)
        p = _matmul(h, w_in_b, l, BF16)
        qd, kd, qr, kr = _rope_prep(p, rope_tabs, rope_tm, seq, n_lat)
        y_hg, yc_hg = _hgrn(p, lower_bounds[0, l], lower_bounds[1, l], hgrn_norm[l], hg_tabs,
                            batch, seq, ctx_len, need_ctx)
        lam_init = _lambda_init_for(l)
        y_da = _diff_attn(qd, kd, p, diff_lambda[l], diff_subln[l], lam_init, batch, seq, ctx_len, True)
        y_rt, yc_rt = _retention(qr, kr, p, rt_tabs, batch, seq, ctx_len, need_ctx)
        ys = [y_hg, y_da, y_rt]
        if need_ctx:
            yc_da = _diff_attn(qd, kd, p, diff_lambda[l], diff_subln[l], lam_init, batch, seq,
                               ctx_len, False)
            ys = [jnp.concatenate([a, b], axis=0) for a, b in zip(ys, (yc_hg, yc_da, yc_rt))]
        merged = _merge(h, ys, w_mgate_b, b_mgate, w_branch_b, l, rows)
        xa = _proj_residual(merged, w_out_b, l, xa, g1, None, rows, seq, n_lat, batch)

        h2, idx_t, gate_t = _norm_router(xa, norm_ffn[l], sc2, sh2, w_router[l], router_bias[l],
                                         rows, seq, n_lat, batch)
        buf_tok, buf_w, pos, block_expert, n_used = _dispatch_plan(idx_t, gate_t, MOE_BM)
        xs = _take_rows(h2, buf_tok)
        y_sorted = _experts(xs, w_exp_gate, w_exp_up, w_exp_down, l, buf_w, block_expert, n_used,
                            MOE_BM)
        y_slots = _take_rows(y_sorted, pos.reshape(TOP_K * rows)).reshape(TOP_K, rows, d)
        up = _swiglu_up(h2, w_sg_b, w_su_b, l)
        xa = _proj_residual(up, w_sd_b, l, xa, g2, y_slots, rows, seq, n_lat, batch)

    zeros = jnp.zeros((8, 1, d), F32)
    out = _norm_mod(xa, norm_final, zeros, zeros, n_lat, seq, n_lat, batch, F32)
    return out.reshape(batch, seq, d)
```

```python
import functools
import math

import numpy as np
import jax
import jax.numpy as jnp
from jax import lax
from jax.experimental import pallas as pl
from jax.experimental.pallas import tpu as pltpu

F32 = jnp.float32
BF16 = jnp.bfloat16

NORM_EPS = 1e-6
ROPE_BASE = 10000.0
GRID_W = 64
HG_HEADS = 8
DA_HEADS = 8
DA_DH = 64
RT_HEADS = 8
RT_DK = 64
N_EXPERTS = 64
TOP_K = 8
N_GROUPS = 8
TOPK_GROUPS = 4
EXPERT_FF = 256
ROUTED_SCALE = 2.5
BRANCH_W = 1024

LANES = 128
V7X_VMEM_BYTES = 64 * 1024 * 1024
VMEM_LIMIT = 56 * 1024 * 1024

COL_HG_Q, COL_HG_FF, COL_HG_FB, COL_HG_I, COL_HG_G = 0, 8, 16, 24, 32
COL_DA_Q, COL_DA_K, COL_DA_V = 40, 48, 56
COL_RT_Q, COL_RT_K, COL_RT_V, COL_RT_G = 64, 68, 72, 80

NT_DIMS = (((1,), (1,)), ((), ()))
TN_DIMS = (((0,), (0,)), ((), ()))

ROW_TILE = 256
MM_TM = 1024
HG_CHUNK = 128
RT_CHUNK = 256
ATT_TQ = 512
ATT_TK = 512
ATT_VROWS = LANES + 16
MOE_BM = 256


def _cparams(*sem):
    return pltpu.CompilerParams(dimension_semantics=sem, vmem_limit_bytes=VMEM_LIMIT)


def _div_tile(n, pref):
    t = min(n, pref)
    while n % t:
        t //= 2
    return t


def _sigmoid(x):
    return 1.0 / (1.0 + jnp.exp(-x))


def _silu(x):
    return x * _sigmoid(x)


def _ada_kernel(c_ref, w_ref, b_ref, o_ref):
    a = _silu(c_ref[...]).astype(BF16)
    o_ref[...] = jnp.dot(a, w_ref[...].astype(BF16), preferred_element_type=F32) + b_ref[...]


def _ada_mod(c8, w_ada, b_ada):
    depth, d, n = w_ada.shape
    tn = _div_tile(n, 512)
    return pl.pallas_call(
        _ada_kernel,
        out_shape=jax.ShapeDtypeStruct((depth, 8, n), F32),
        grid=(depth, n // tn),
        in_specs=[
            pl.BlockSpec((8, d), lambda l, j: (0, 0)),
            pl.BlockSpec((None, d, tn), lambda l, j: (l, 0, j)),
            pl.BlockSpec((None, 1, tn), lambda l, j: (l, 0, j)),
        ],
        out_specs=pl.BlockSpec((None, 8, tn), lambda l, j: (l, 0, j)),
        compiler_params=_cparams("arbitrary", "arbitrary"),
        name="ada_mod",
    )(c8, w_ada, b_ada.reshape(depth, 1, n))


def _mod_row_map(tm, seq, n_lat, batch):
    def index_map(i):
        start = i * tm
        return (jnp.where(start < n_lat, start // seq, batch), 0, 0)
    return index_map


def _norm_mod_kernel(x_ref, w_ref, sc_ref, sh_ref, o_ref):
    x = x_ref[...]
    ms = jnp.mean(x * x, axis=-1, keepdims=True)
    y = x * lax.rsqrt(ms + NORM_EPS) * w_ref[...]
    o_ref[...] = (y * (1.0 + sc_ref[...]) + sh_ref[...]).astype(o_ref.dtype)


def _norm_mod(x, w, sc, sh, rows, seq, n_lat, batch, out_dtype):
    d = x.shape[1]
    tm = _div_tile(math.gcd(seq, rows), ROW_TILE)
    rmap = _mod_row_map(tm, seq, n_lat, batch)
    return pl.pallas_call(
        _norm_mod_kernel,
        out_shape=jax.ShapeDtypeStruct((rows, d), out_dtype),
        grid=(rows // tm,),
        in_specs=[
            pl.BlockSpec((tm, d), lambda i: (i, 0)),
            pl.BlockSpec((1, d), lambda i: (0, 0)),
            pl.BlockSpec((None, 1, d), rmap),
            pl.BlockSpec((None, 1, d), rmap),
        ],
        out_specs=pl.BlockSpec((tm, d), lambda i: (i, 0)),
        compiler_params=_cparams("parallel"),
        name="norm_mod",
    )(x, w.reshape(1, d), sc, sh)


def _mm_kernel(a_ref, b_ref, o_ref):
    o_ref[...] = jnp.dot(a_ref[...], b_ref[...], preferred_element_type=F32).astype(o_ref.dtype)


def _matmul(a, b, layer, out_dtype, tn_pref=1024):
    m, k = a.shape
    n = b.shape[2]
    tm = _div_tile(m, MM_TM)
    tn = _div_tile(n, tn_pref)
    return pl.pallas_call(
        _mm_kernel,
        out_shape=jax.ShapeDtypeStruct((m, n), out_dtype),
        grid=(m // tm, n // tn),
        in_specs=[pl.BlockSpec((tm, k), lambda i, j: (i, 0)),
                  pl.BlockSpec((None, k, tn), lambda i, j: (layer, 0, j))],
        out_specs=pl.BlockSpec((tm, tn), lambda i, j: (i, j)),
        compiler_params=_cparams("parallel", "arbitrary"),
        name="matmul",
    )(a, b)


def _rope_tables(seq, n_ctx):
    pos = np.arange(seq, dtype=np.float64)
    rows = np.floor(pos / GRID_W)
    cols = pos - rows * GRID_W
    lane = np.arange(LANES)
    j = lane % 64
    part, jj = j // 32, j % 32
    inv = ROPE_BASE ** (-np.arange(0, 32, 2, dtype=np.float64) / 32.0)
    ang = np.where(part[None, :] == 0, rows[:, None], cols[:, None]) * inv[jj % 16][None, :]
    cos_da = np.cos(ang)
    sin_da = np.sin(ang) * np.where(jj < 16, -1.0, 1.0)[None, :]
    partner_da = np.where(jj < 16, lane + 16, lane - 16)
    theta = 1.0 / (ROPE_BASE ** np.linspace(0.0, 1.0, 32))
    ang = pos[:, None] * theta[j % 32][None, :]
    cos_rt = np.cos(ang)
    sin_rt = np.sin(ang) * np.where(j < 32, -1.0, 1.0)[None, :]
    partner_rt = np.where(j < 32, lane + 32, lane - 32)

    def full(tab, fill):
        return jnp.asarray(np.concatenate([tab, np.full((n_ctx, LANES), fill)], 0), F32)

    def perm(partner):
        p = np.zeros((LANES, LANES), np.float32)
        p[partner, lane] = 1.0
        return jnp.asarray(p, BF16)

    return (full(cos_da, 1.0), full(sin_da, 0.0), full(cos_rt, 1.0), full(sin_rt, 0.0),
            perm(partner_da), perm(partner_rt))


def _rope_kernel(qd_ref, kd_ref, qr_ref, kr_ref, cd_ref, sd_ref, cr_ref, sr_ref, pd_ref, pr_ref,
                 oqd_ref, okd_ref, oqr_ref, okr_ref):
    def rot(src, dst, cos_ref, sin_ref, perm_ref, scale):
        cos = cos_ref[...]
        sin = sin_ref[...]
        perm = perm_ref[...]
        for g in range(src.shape[1] // LANES):
            sl = slice(g * LANES, (g + 1) * LANES)
            x = src[:, sl]
            xs = jnp.dot(x, perm, preferred_element_type=F32)
            y = x.astype(F32) * cos + xs * sin
            dst[:, sl] = (y * scale).astype(dst.dtype)

    rot(qd_ref, oqd_ref, cd_ref, sd_ref, pd_ref, DA_DH ** -0.5 * math.log2(math.e))
    rot(kd_ref, okd_ref, cd_ref, sd_ref, pd_ref, 1.0)
    rot(qr_ref, oqr_ref, cr_ref, sr_ref, pr_ref, 1.0)
    rot(kr_ref, okr_ref, cr_ref, sr_ref, pr_ref, RT_DK ** -0.5)


def _rope_prep(p, tables, tm, seq, n_lat):
    rows = p.shape[0]
    cos_da, sin_da, cos_rt, sin_rt, perm_da, perm_rt = tables
    lat_tiles, seq_tiles = n_lat // tm, seq // tm

    def tmap(i):
        return (jnp.where(i < lat_tiles, i % seq_tiles, seq_tiles), 0)

    wd, wr = DA_HEADS * 2 * DA_DH, RT_HEADS * RT_DK
    tab = pl.BlockSpec((tm, LANES), tmap)
    pm = pl.BlockSpec((LANES, LANES), lambda i: (0, 0))
    return pl.pallas_call(
        _rope_kernel,
        out_shape=(jax.ShapeDtypeStruct((rows, wd), BF16), jax.ShapeDtypeStruct((rows, wd), BF16),
                   jax.ShapeDtypeStruct((rows, wr), BF16), jax.ShapeDtypeStruct((rows, wr), BF16)),
        grid=(rows // tm,),
        in_specs=[
            pl.BlockSpec((tm, wd), lambda i: (i, COL_DA_Q * LANES // wd)),
            pl.BlockSpec((tm, wd), lambda i: (i, COL_DA_K * LANES // wd)),
            pl.BlockSpec((tm, wr), lambda i: (i, COL_RT_Q * LANES // wr)),
            pl.BlockSpec((tm, wr), lambda i: (i, COL_RT_K * LANES // wr)),
            tab, tab, tab, tab, pm, pm,
        ],
        out_specs=(pl.BlockSpec((tm, wd), lambda i: (i, 0)), pl.BlockSpec((tm, wd), lambda i: (i, 0)),
                   pl.BlockSpec((tm, wr), lambda i: (i, 0)), pl.BlockSpec((tm, wr), lambda i: (i, 0))),
        compiler_params=_cparams("parallel"),
        name="rope_prep",
    )(p, p, p, p, cos_da, sin_da, cos_rt, sin_rt, perm_da, perm_rt)


def _hgrn_tables(c):
    nlev = int(math.log2(c))
    m = np.zeros((nlev + 2, c, c), np.float32)
    w = np.zeros((nlev + 1, c, c), np.float32)
    for l in range(nlev):
        bs = c >> (l + 1)
        for t in range(c):
            blk = t // bs
            if blk % 2 == 1:
                m[l, t, blk * bs:t + 1] = 1.0
                w[l, t, (blk - 1) * bs:blk * bs] = 1.0
            else:
                m[l, t, t + 1:(blk + 1) * bs] = 1.0
    for t in range(c):
        m[nlev, t, :t + 1] = 1.0
        m[nlev + 1, t, t + 1:] = 1.0
    w[nlev] = np.eye(c)
    mb = m[:, ::-1, ::-1]
    wb = w[:, ::-1, ::-1]
    r = (nlev + 2) * c
    return (jnp.asarray(m.reshape(r, c), BF16), jnp.asarray(mb.reshape(r, c), BF16),
            jnp.asarray(w, F32), jnp.asarray(wb, F32))


def _hgrn_chunk(q_raw, f_raw, v, lb, m_ref, w_ref, st, fwd):
    c = q_raw.shape[0]
    nlev = w_ref.shape[0] - 1
    q = _silu(q_raw.astype(F32))
    f = lb + (1.0 - lb) * _sigmoid(f_raw.astype(F32))
    lf = jnp.log(f)
    k = 1.0 - f
    hi = lf.astype(BF16)
    lo = (lf - hi.astype(F32)).astype(BF16)
    g2 = jnp.dot(m_ref[...], jnp.concatenate([hi, lo], axis=1), preferred_element_type=F32)
    g = g2[:, :LANES] + g2[:, LANES:]
    e = jnp.exp(g)
    row = lax.broadcasted_iota(jnp.int32, (c, LANES), 0)
    a = w_ref[nlev] * lax.dot_general(q.astype(BF16), k.astype(BF16), NT_DIMS,
                                      preferred_element_type=F32)
    for l in range(nlev):
        shift = int(math.log2(c >> (l + 1)))
        odd = ((row >> shift) & 1) == 1
        x = (jnp.where(odd, q, k) if fwd else jnp.where(odd, k, q)) * e[l * c:(l + 1) * c]
        xb = x.astype(BF16)
        a = a + w_ref[l] * lax.dot_general(xb, xb, NT_DIMS, preferred_element_type=F32)
    qe = (q * e[nlev * c:(nlev + 1) * c]).astype(BF16)
    ke = (k * e[(nlev + 1) * c:(nlev + 2) * c]).astype(BF16)
    o = (jnp.dot(a.astype(BF16), v, preferred_element_type=F32)
         + lax.dot_general(qe, st.astype(BF16), NT_DIMS, preferred_element_type=F32))
    last = nlev * c + (c - 1 if fwd else 0)
    dec = jnp.exp(g[last:last + 1, :])
    st_new = dec * st + lax.dot_general(v, ke, TN_DIMS, preferred_element_type=F32)
    return o, st_new


def _hgrn_kernel(need_ctx, chunk, *refs):
    (ql, ffl, fbl, il, gl, qc, ffc, fbc, ic, gc, lbf_ref, lbb_ref, nw_ref,
     mf_ref, mb_ref, wf_ref, wb_ref) = refs[:17]
    rest = refs[17:]
    if need_ctx:
        yl_ref, yc_ref, of_ref, ob_ref, ocf_ref, ocb_ref, st_ref = rest
    else:
        yl_ref, of_ref, ob_ref, st_ref = rest
        yc_ref = ocf_ref = ocb_ref = None
    nc_lat = ql.shape[0] // chunk
    nc_ctx = qc.shape[0] // chunk
    lbf = lbf_ref[...]
    lbb = lbb_ref[...]
    st_ref[...] = jnp.zeros_like(st_ref)

    def step(j, q_ref, ff_ref, fb_ref, i_ref, n, outf, outb):
        sf = pl.ds(pl.multiple_of(j * chunk, chunk), chunk)
        sb = pl.ds(pl.multiple_of((n - 1 - j) * chunk, chunk), chunk)
        o, s = _hgrn_chunk(q_ref[sf, :], ff_ref[sf, :], i_ref[sf, :], lbf, mf_ref, wf_ref,
                           st_ref[0], True)
        st_ref[0] = s
        if outf is not None:
            outf[sf, :] = o
        o, s = _hgrn_chunk(q_ref[sb, :], fb_ref[sb, :], i_ref[sb, :], lbb, mb_ref, wb_ref,
                           st_ref[1], False)
        st_ref[1] = s
        if outb is not None:
            outb[sb, :] = o

    def ctx_body(j, carry):
        step(j, qc, ffc, fbc, ic, nc_ctx, ocf_ref, ocb_ref)
        return carry

    def lat_body(j, carry):
        step(j, ql, ffl, fbl, il, nc_lat, of_ref, ob_ref)
        return carry

    lax.fori_loop(0, nc_ctx, ctx_body, 0)
    lax.fori_loop(0, nc_lat, lat_body, 0, unroll=2)

    nw = nw_ref[...]

    def readout(n, a_ref, b_ref, g_ref, y_ref):
        def body(j, carry):
            s = pl.ds(pl.multiple_of(j * chunk, chunk), chunk)
            o = a_ref[s, :] + b_ref[s, :]
            o = o * lax.rsqrt(jnp.mean(o * o, axis=-1, keepdims=True) + NORM_EPS) * nw
            y_ref[s, :] = (o * _silu(g_ref[s, :].astype(F32))).astype(y_ref.dtype)
            return carry
        lax.fori_loop(0, n, body, 0)

    readout(nc_lat, of_ref, ob_ref, gl, yl_ref)
    if need_ctx:
        readout(nc_ctx, ocf_ref, ocb_ref, gc, yc_ref)


def _hgrn(p, lb_f, lb_b, norm_w, tables, batch, seq, ctx_len, need_ctx):
    n_lat = batch * seq
    chunk = _div_tile(math.gcd(seq, ctx_len), HG_CHUNK)
    mf, mb, wf, wb = tables
    ctx_blk0 = n_lat // ctx_len
    width = HG_HEADS * LANES

    def lat(col):
        return pl.BlockSpec((seq, LANES), lambda b, h: (b, col + h))

    def ctx(col):
        return pl.BlockSpec((ctx_len, LANES), lambda b, h: (ctx_blk0 + b, col + h))

    head_vec = pl.BlockSpec((1, LANES), lambda b, h: (0, h))
    const2 = lambda arr: pl.BlockSpec(arr.shape, lambda b, h: (0,) * arr.ndim)
    cols = (COL_HG_Q, COL_HG_FF, COL_HG_FB, COL_HG_I, COL_HG_G)
    in_specs = ([lat(c) for c in cols] + [ctx(c) for c in cols]
                + [head_vec, head_vec, pl.BlockSpec((1, LANES), lambda b, h: (0, 0)),
                   const2(mf), const2(mb), const2(wf), const2(wb)])
    out_shape = [jax.ShapeDtypeStruct((n_lat, width), BF16)]
    out_specs = [pl.BlockSpec((seq, LANES), lambda b, h: (b, h))]
    scratch = [pltpu.VMEM((seq, LANES), F32), pltpu.VMEM((seq, LANES), F32)]
    if need_ctx:
        out_shape.append(jax.ShapeDtypeStruct((batch * ctx_len, width), BF16))
        out_specs.append(pl.BlockSpec((ctx_len, LANES), lambda b, h: (b, h)))
        scratch += [pltpu.VMEM((ctx_len, LANES), F32), pltpu.VMEM((ctx_len, LANES), F32)]
    scratch.append(pltpu.VMEM((2, LANES, LANES), F32))
    outs = pl.pallas_call(
        functools.partial(_hgrn_kernel, need_ctx, chunk),
        out_shape=tuple(out_shape),
        grid=(batch, HG_HEADS),
        in_specs=in_specs,
        out_specs=tuple(out_specs),
        scratch_shapes=scratch,
        compiler_params=_cparams("parallel", "parallel"),
        name="hgrn2_scan",
    )(*([p] * 10), lb_f.reshape(1, width), lb_b.reshape(1, width), norm_w.reshape(1, LANES),
      mf, mb, wf, wb)
    return outs[0], (outs[1] if need_ctx else None)


def _attn_kernel(has_lat, tk, lambda_init, *refs):
    if has_lat:
        (q_ref, kc_ref, vc_ref, kl_ref, vl_ref, lam_ref, sub_ref, o_ref,
         s_ref, vtc_ref, vtl_ref) = refs
    else:
        q_ref, kc_ref, vc_ref, lam_ref, sub_ref, o_ref, s_ref, vtc_ref = refs
    tq = q_ref.shape[0]
    w2 = 2 * tq
    nctx = kc_ref.shape[0]
    nlat = kl_ref.shape[0] if has_lat else 0
    n_chunks = nlat // tk
    sub_t = min(256, tk)

    @pl.when(pl.program_id(2) == 0)
    def _():
        vtc_ref[LANES:, :] = jnp.ones((ATT_VROWS - LANES, nctx), BF16)
        for r in range(0, nctx, min(sub_t, nctx)):
            rr = slice(r, r + min(sub_t, nctx))
            vtc_ref[:LANES, rr] = vc_ref[rr, :].astype(F32).T.astype(BF16)
        for j in range(n_chunks):
            vtl_ref[j, LANES:, :] = jnp.ones((ATT_VROWS - LANES, tk), BF16)
            for r in range(0, tk, sub_t):
                vtl_ref[j, :LANES, r:r + sub_t] = (
                    vl_ref[j * tk + r:j * tk + r + sub_t, :].astype(F32).T.astype(BF16))

    q = q_ref[...].astype(F32).T
    sub = lax.broadcasted_iota(jnp.int32, (LANES, tq), 0)
    qst = jnp.concatenate([jnp.where(sub < DA_DH, q, 0.0), jnp.where(sub >= DA_DH, q, 0.0)],
                          axis=1).astype(BF16)

    def col_fold(x, op):
        return op(x.reshape(x.shape[0] // 8, 8, w2), axis=0)

    def scores(k_ref, src, dst, macc):
        s = jnp.dot(k_ref[src, :], qst, preferred_element_type=F32)
        s_ref[dst, :] = s
        return jnp.maximum(macc, col_fold(s, jnp.max))

    macc = scores(kc_ref, slice(0, nctx), slice(0, nctx), jnp.full((8, w2), -jnp.inf, F32))
    for j in range(n_chunks):
        macc = scores(kl_ref, slice(j * tk, (j + 1) * tk), slice(nctx + j * tk, nctx + (j + 1) * tk),
                      macc)
    m = jnp.max(macc, axis=0, keepdims=True)

    def weights(rows):
        return jnp.exp2((s_ref[rows, :] - m).astype(BF16))

    pv = jnp.dot(vtc_ref[...], weights(slice(0, nctx)), preferred_element_type=F32)
    for j in range(n_chunks):
        pv = pv + jnp.dot(vtl_ref[j], weights(slice(nctx + j * tk, nctx + (j + 1) * tk)),
                          preferred_element_type=F32)
    on = pv[:LANES] / pv[LANES:LANES + 1]
    lv = lam_ref[...]
    lam = (jnp.exp(jnp.sum(lv[0:1] * lv[1:2], axis=-1, keepdims=True))
           - jnp.exp(jnp.sum(lv[2:3] * lv[3:4], axis=-1, keepdims=True)) + lambda_init)
    o = on[:, :tq] - lam * on[:, tq:]
    o = o * lax.rsqrt(jnp.mean(o * o, axis=0, keepdims=True) + NORM_EPS) * sub_ref[...]
    o_ref[...] = (o * (1.0 - lambda_init)).T.astype(o_ref.dtype)


def _diff_attn(qd, kd, p, lam_vec, subln_w, lambda_init, batch, seq, ctx_len, latent_queries):
    n_lat = batch * seq
    ctx_blk0 = n_lat // ctx_len
    width = DA_HEADS * LANES
    lq = seq if latent_queries else ctx_len
    tq = _div_tile(lq, ATT_TQ)
    q_blk0 = 0 if latent_queries else n_lat // tq
    qt = lq // tq
    tk = _div_tile(seq, ATT_TK)
    in_specs = [
        pl.BlockSpec((tq, LANES), lambda b, h, i: (q_blk0 + b * qt + i, h)),
        pl.BlockSpec((ctx_len, LANES), lambda b, h, i: (ctx_blk0 + b, h)),
        pl.BlockSpec((ctx_len, LANES), lambda b, h, i: (ctx_blk0 + b, COL_DA_V + h)),
    ]
    args = [qd, kd, p]
    n_keys = ctx_len + (seq if latent_queries else 0)
    scratch = [pltpu.VMEM((n_keys, 2 * tq), F32), pltpu.VMEM((ATT_VROWS, ctx_len), BF16)]
    if latent_queries:
        in_specs += [pl.BlockSpec((seq, LANES), lambda b, h, i: (b, h)),
                     pl.BlockSpec((seq, LANES), lambda b, h, i: (b, COL_DA_V + h))]
        args += [kd, p]
        scratch.append(pltpu.VMEM((seq // tk, ATT_VROWS, tk), BF16))
    in_specs += [pl.BlockSpec((4, DA_DH), lambda b, h, i: (0, 0)),
                 pl.BlockSpec((LANES, 1), lambda b, h, i: (0, 0))]
    args += [lam_vec, subln_w.reshape(LANES, 1)]
    return pl.pallas_call(
        functools.partial(_attn_kernel, latent_queries, tk, lambda_init),
        out_shape=jax.ShapeDtypeStruct((batch * lq, width), BF16),
        grid=(batch, DA_HEADS, qt),
        in_specs=in_specs,
        out_specs=pl.BlockSpec((tq, LANES), lambda b, h, i: (b * qt + i, h)),
        scratch_shapes=scratch,
        compiler_params=_cparams("parallel", "parallel", "arbitrary"),
        name="diff_attn_lat" if latent_queries else "diff_attn_ctx",
    )(*args)


def _ret_tables(c):
    gam = 1.0 - 2.0 ** (-5.0 - np.arange(RT_HEADS, dtype=np.float64))
    idx = np.arange(c, dtype=np.float64)
    dist = np.abs(idx[:, None] - idx[None, :])
    dsym = gam[:, None, None] ** dist[None] * np.where(dist == 0, 2.0, 1.0)[None]
    lane_head = np.arange(LANES) // RT_DK
    mask = (lane_head[None, :] == (np.arange(RT_HEADS) % 2)[:, None]).astype(np.float64)

    def tab(power):
        return (gam[:, None] ** power[None, :])[:, :, None] * mask[:, None, :]

    qdf, kdf = tab(idx + 1.0), tab(c - 1.0 - idx)
    qdb, kdb = tab(c - idx), tab(idx)
    cdec = np.broadcast_to((gam ** c)[:, None, None], (RT_HEADS, 1, LANES))
    hp = RT_HEADS // 2
    f = lambda x: jnp.asarray(np.ascontiguousarray(x).reshape((hp, 2) + x.shape[1:]), F32)
    return f(dsym), f(mask[:, None, :]), f(qdf), f(kdf), f(qdb), f(kdb), f(cdec)


def _ret_kernel(need_ctx, chunk, *refs):
    (ql, kl, vl, gl, qc, kc, vc, gc, dsym_ref, msk_ref, qdf_ref, kdf_ref, qdb_ref, kdb_ref,
     cdec_ref) = refs[:15]
    rest = refs[15:]
    if need_ctx:
        yl_ref, yc_ref, of_ref, ob_ref, ocf_ref, ocb_ref, st_ref = rest
    else:
        yl_ref, of_ref, ob_ref, st_ref = rest
        yc_ref = ocf_ref = ocb_ref = None
    nc_lat = ql.shape[0] // chunk
    nc_ctx = qc.shape[0] // chunk
    st_ref[...] = jnp.zeros_like(st_ref)

    def step(j, q_ref, k_ref, v_ref, n, outf, outb):
        sf = pl.ds(pl.multiple_of(j * chunk, chunk), chunk)
        sb = pl.ds(pl.multiple_of((n - 1 - j) * chunk, chunk), chunk)
        qf = q_ref[sf, :].astype(F32)
        kfb = k_ref[sf, :]
        kf = kfb.astype(F32)
        qb = q_ref[sb, :].astype(F32)
        kb = k_ref[sb, :].astype(F32)
        for h in range(2):
            hs = slice(h * LANES, (h + 1) * LANES)
            vf = v_ref[sf, hs]
            vb = v_ref[sb, hs]
            a = lax.dot_general((qf * msk_ref[h]).astype(BF16), kfb, NT_DIMS,
                                preferred_element_type=F32) * dsym_ref[h]
            o = (jnp.dot(a.astype(BF16), vf, preferred_element_type=F32)
                 + jnp.dot((qf * qdf_ref[h]).astype(BF16), st_ref[h].astype(BF16),
                           preferred_element_type=F32))
            st_ref[h] = cdec_ref[h] * st_ref[h] + lax.dot_general(
                (kf * kdf_ref[h]).astype(BF16), vf, TN_DIMS, preferred_element_type=F32)
            if outf is not None:
                outf[sf, hs] = o
            o = jnp.dot((qb * qdb_ref[h]).astype(BF16), st_ref[2 + h].astype(BF16),
                        preferred_element_type=F32)
            st_ref[2 + h] = cdec_ref[h] * st_ref[2 + h] + lax.dot_general(
                (kb * kdb_ref[h]).astype(BF16), vb, TN_DIMS, preferred_element_type=F32)
            if outb is not None:
                outb[sb, hs] = o

    def ctx_body(j, carry):
        step(j, qc, kc, vc, nc_ctx, ocf_ref, ocb_ref)
        return carry

    def lat_body(j, carry):
        step(j, ql, kl, vl, nc_lat, of_ref, ob_ref)
        return carry

    lax.fori_loop(0, nc_ctx, ctx_body, 0)
    lax.fori_loop(0, nc_lat, lat_body, 0)

    def readout(n, a_ref, b_ref, g_ref, y_ref):
        def body(j, carry):
            s = pl.ds(pl.multiple_of(j * chunk, chunk), chunk)
            for h in range(2):
                hs = slice(h * LANES, (h + 1) * LANES)
                o = a_ref[s, hs] + b_ref[s, hs]
                o = o * lax.rsqrt(jnp.mean(o * o, axis=-1, keepdims=True) + NORM_EPS)
                y_ref[s, hs] = (o * _silu(g_ref[s, hs].astype(F32))).astype(y_ref.dtype)
            return carry
        lax.fori_loop(0, n, body, 0)

    readout(nc_lat, of_ref, ob_ref, gl, yl_ref)
    if need_ctx:
        readout(nc_ctx, ocf_ref, ocb_ref, gc, yc_ref)


def _retention(qr, kr, p, tables, batch, seq, ctx_len, need_ctx):
    n_lat = batch * seq
    chunk = _div_tile(math.gcd(seq, ctx_len), RT_CHUNK)
    ctx_blk0 = n_lat // ctx_len
    hp = RT_HEADS // 2
    pair = 2 * LANES
    width = RT_HEADS * LANES
    v_col, g_col = COL_RT_V * LANES // pair, COL_RT_G * LANES // pair

    in_specs = [
        pl.BlockSpec((seq, LANES), lambda b, h: (b, h)),
        pl.BlockSpec((seq, LANES), lambda b, h: (b, h)),
        pl.BlockSpec((seq, pair), lambda b, h: (b, v_col + h)),
        pl.BlockSpec((seq, pair), lambda b, h: (b, g_col + h)),
        pl.BlockSpec((ctx_len, LANES), lambda b, h: (ctx_blk0 + b, h)),
        pl.BlockSpec((ctx_len, LANES), lambda b, h: (ctx_blk0 + b, h)),
        pl.BlockSpec((ctx_len, pair), lambda b, h: (ctx_blk0 + b, v_col + h)),
        pl.BlockSpec((ctx_len, pair), lambda b, h: (ctx_blk0 + b, g_col + h)),
    ]
    for t in tables:
        in_specs.append(pl.BlockSpec((None,) + t.shape[1:], lambda b, h: (h, 0, 0, 0)))
    out_shape = [jax.ShapeDtypeStruct((n_lat, width), BF16)]
    out_specs = [pl.BlockSpec((seq, pair), lambda b, h: (b, h))]
    scratch = [pltpu.VMEM((seq, pair), F32), pltpu.VMEM((seq, pair), F32)]
    if need_ctx:
        out_shape.append(jax.ShapeDtypeStruct((batch * ctx_len, width), BF16))
        out_specs.append(pl.BlockSpec((ctx_len, pair), lambda b, h: (b, h)))
        scratch += [pltpu.VMEM((ctx_len, pair), F32), pltpu.VMEM((ctx_len, pair), F32)]
    scratch.append(pltpu.VMEM((4, LANES, LANES), F32))
    outs = pl.pallas_call(
        functools.partial(_ret_kernel, need_ctx, chunk),
        out_shape=tuple(out_shape),
        grid=(batch, hp),
        in_specs=in_specs,
        out_specs=tuple(out_specs),
        scratch_shapes=scratch,
        compiler_params=_cparams("parallel", "parallel"),
        name="retention_scan",
    )(qr, kr, p, p, qr, kr, p, p, *tables)
    return outs[0], (outs[1] if need_ctx else None)


def _merge_kernel(h_ref, y0_ref, y1_ref, y2_ref, wg_ref, bg_ref, wb_ref, o_ref):
    h = h_ref[...]
    acc = None
    for i, y_ref in enumerate((y0_ref, y1_ref, y2_ref)):
        gate = _sigmoid(jnp.dot(h, wg_ref[i], preferred_element_type=F32) + bg_ref[i])
        t = gate * jnp.dot(y_ref[...], wb_ref[i], preferred_element_type=F32)
        acc = t if acc is None else acc + t
    o_ref[...] = acc.astype(o_ref.dtype)


def _merge(h, ys, w_mgate, b_mgate, w_branch, layer, rows):
    d = h.shape[1]
    tm = _div_tile(rows, MM_TM)
    tn = _div_tile(d, 256)
    ysp = pl.BlockSpec((tm, BRANCH_W), lambda i, j: (i, 0))
    return pl.pallas_call(
        _merge_kernel,
        out_shape=jax.ShapeDtypeStruct((rows, d), BF16),
        grid=(rows // tm, d // tn),
        in_specs=[pl.BlockSpec((tm, d), lambda i, j: (i, 0)), ysp, ysp, ysp,
                  pl.BlockSpec((None, 3, d, tn), lambda i, j: (layer, 0, 0, j)),
                  pl.BlockSpec((None, 3, 1, tn), lambda i, j: (layer, 0, 0, j)),
                  pl.BlockSpec((None, 3, BRANCH_W, tn), lambda i, j: (layer, 0, 0, j))],
        out_specs=pl.BlockSpec((tm, tn), lambda i, j: (i, j)),
        compiler_params=_cparams("parallel", "arbitrary"),
        name="branch_merge",
    )(h, *ys, w_mgate, b_mgate.reshape(b_mgate.shape[0], 3, 1, d), w_branch)


def _proj_res_kernel(n_extra, *refs):
    a_ref, w_ref, x_ref, g_ref = refs[:4]
    o_ref = refs[4 + n_extra]
    y = jnp.dot(a_ref[...], w_ref[...], preferred_element_type=F32)
    for e_ref in refs[4:4 + n_extra]:
        y = y + e_ref[...].astype(F32)
    o_ref[...] = x_ref[...] + g_ref[...] * y


def _proj_residual(a, w, layer, x, gate, extra, n_extra, rows, seq, n_lat, batch):
    k = a.shape[1]
    d = w.shape[2]
    tm = _div_tile(math.gcd(seq, rows), MM_TM)
    tn = _div_tile(d, 512)
    rmap = _mod_row_map(tm, seq, n_lat, batch)
    in_specs = [pl.BlockSpec((tm, k), lambda i, j: (i, 0)),
                pl.BlockSpec((None, k, tn), lambda i, j: (layer, 0, j)),
                pl.BlockSpec((tm, tn), lambda i, j: (i, j)),
                pl.BlockSpec((None, 1, tn), lambda i, j: rmap(i)[:2] + (j,))]
    args = [a, w, x, gate]
    col_blocks = d // tn
    for s in range(n_extra):
        in_specs.append(pl.BlockSpec((tm, tn), lambda i, j, s=s: (i, s * col_blocks + j)))
        args.append(extra)
    return pl.pallas_call(
        functools.partial(_proj_res_kernel, n_extra),
        out_shape=jax.ShapeDtypeStruct((rows, d), F32),
        grid=(rows // tm, d // tn),
        in_specs=in_specs,
        out_specs=pl.BlockSpec((tm, tn), lambda i, j: (i, j)),
        compiler_params=_cparams("parallel", "arbitrary"),
        name="proj_residual",
    )(*args)


def _router_kernel(x_ref, w_ref, sc_ref, sh_ref, wr_ref, rb_ref, h_ref, idx_ref, gate_ref):
    x = x_ref[...]
    ms = jnp.mean(x * x, axis=-1, keepdims=True)
    h = x * lax.rsqrt(ms + NORM_EPS) * w_ref[...] * (1.0 + sc_ref[...]) + sh_ref[...]
    h_ref[...] = h.astype(h_ref.dtype)
    tm = x.shape[0]
    logits = lax.dot_general(wr_ref[...], h, NT_DIMS, precision=lax.Precision.HIGHEST,
                             preferred_element_type=F32)
    scores = _sigmoid(logits)
    choice = scores + rb_ref[...]
    per_group = N_EXPERTS // N_GROUPS
    neg = -jnp.inf
    sub = lax.broadcasted_iota(jnp.int32, (per_group, tm), 0).astype(F32)
    gs = []
    for g in range(N_GROUPS):
        xg = choice[g * per_group:(g + 1) * per_group, :]
        m1 = jnp.max(xg, axis=0, keepdims=True)
        i1 = jnp.min(jnp.where(xg == m1, sub, float(per_group)), axis=0, keepdims=True)
        m2 = jnp.max(jnp.where(sub == i1, neg, xg), axis=0, keepdims=True)
        gs.append(m1 + m2)
    sel = [jnp.zeros((1, tm), F32) for _ in range(N_GROUPS)]
    for _ in range(TOPK_GROUPS):
        m = functools.reduce(jnp.maximum, gs)
        found = jnp.zeros((1, tm), F32)
        for g in range(N_GROUPS):
            hit = jnp.where(gs[g] == m, 1.0 - found, 0.0)
            sel[g] = sel[g] + hit
            found = found + hit
            gs[g] = jnp.where(hit > 0.5, neg, gs[g])
    masked = jnp.concatenate(
        [jnp.where(sel[g] > 0.5, choice[g * per_group:(g + 1) * per_group, :], neg)
         for g in range(N_GROUPS)], axis=0)
    eidx = lax.broadcasted_iota(jnp.int32, (N_EXPERTS, tm), 0).astype(F32)
    idxs, ws = [], []
    for _ in range(TOP_K):
        m = jnp.max(masked, axis=0, keepdims=True)
        i = jnp.min(jnp.where(masked == m, eidx, float(N_EXPERTS)), axis=0, keepdims=True)
        hit = eidx == i
        ws.append(jnp.sum(jnp.where(hit, scores, 0.0), axis=0, keepdims=True))
        idxs.append(i)
        masked = jnp.where(hit, neg, masked)
    wsum = functools.reduce(lambda a, b: a + b, ws)
    idx_ref[...] = jnp.concatenate(idxs, axis=0).astype(jnp.int32)
    gate_ref[...] = jnp.concatenate(ws, axis=0) / wsum * ROUTED_SCALE


def _norm_router(x, w, sc, sh, w_router, router_bias, rows, seq, n_lat, batch):
    d = x.shape[1]
    tm = _div_tile(math.gcd(seq, rows), ROW_TILE)
    rmap = _mod_row_map(tm, seq, n_lat, batch)
    return pl.pallas_call(
        _router_kernel,
        out_shape=(jax.ShapeDtypeStruct((rows, d), BF16),
                   jax.ShapeDtypeStruct((TOP_K, rows), jnp.int32),
                   jax.ShapeDtypeStruct((TOP_K, rows), F32)),
        grid=(rows // tm,),
        in_specs=[pl.BlockSpec((tm, d), lambda i: (i, 0)),
                  pl.BlockSpec((1, d), lambda i: (0, 0)),
                  pl.BlockSpec((None, 1, d), rmap),
                  pl.BlockSpec((None, 1, d), rmap),
                  pl.BlockSpec((N_EXPERTS, d), lambda i: (0, 0)),
                  pl.BlockSpec((N_EXPERTS, 1), lambda i: (0, 0))],
        out_specs=(pl.BlockSpec((tm, d), lambda i: (i, 0)),
                   pl.BlockSpec((TOP_K, tm), lambda i: (0, i)),
                   pl.BlockSpec((TOP_K, tm), lambda i: (0, i))),
        compiler_params=_cparams("parallel"),
        name="norm_router",
    )(x, w.reshape(1, d), sc, sh, w_router.T, router_bias.reshape(N_EXPERTS, 1))


def _expert_kernel(be_ref, nu_ref, x_ref, wg_ref, wu_ref, wd_ref, wt_ref, o_ref,
                   wgb_ref, wub_ref, wdb_ref):
    i = pl.program_id(0)

    @pl.when((i == 0) | (be_ref[i] != be_ref[jnp.maximum(i - 1, 0)]))
    def _():
        wgb_ref[...] = wg_ref[...].astype(BF16)
        wub_ref[...] = wu_ref[...].astype(BF16)
        wdb_ref[...] = wd_ref[...].astype(BF16)

    @pl.when(i < nu_ref[0])
    def _():
        x = x_ref[...]
        g = jnp.dot(x, wgb_ref[...], preferred_element_type=F32)
        u = jnp.dot(x, wub_ref[...], preferred_element_type=F32)
        act = (_silu(g) * u).astype(BF16)
        y = jnp.dot(act, wdb_ref[...], preferred_element_type=F32)
        o_ref[...] = (y * wt_ref[...]).astype(o_ref.dtype)

    @pl.when(i >= nu_ref[0])
    def _():
        o_ref[...] = jnp.zeros_like(o_ref)


def _experts(xs, w_gate, w_up, w_down, layer, buf_w, block_expert, n_used, bm):
    rows, d = xs.shape
    nb = rows // bm
    ff = w_gate.shape[3]
    grid_spec = pltpu.PrefetchScalarGridSpec(
        num_scalar_prefetch=2,
        grid=(nb,),
        in_specs=[pl.BlockSpec((bm, d), lambda i, be, nu: (i, 0)),
                  pl.BlockSpec((None, None, d, ff), lambda i, be, nu: (layer, be[i], 0, 0)),
                  pl.BlockSpec((None, None, d, ff), lambda i, be, nu: (layer, be[i], 0, 0)),
                  pl.BlockSpec((None, None, ff, d), lambda i, be, nu: (layer, be[i], 0, 0)),
                  pl.BlockSpec((bm, 1), lambda i, be, nu: (i, 0))],
        out_specs=pl.BlockSpec((bm, d), lambda i, be, nu: (i, 0)),
        scratch_shapes=[pltpu.VMEM((d, ff), BF16), pltpu.VMEM((d, ff), BF16),
                        pltpu.VMEM((ff, d), BF16)],
    )
    return pl.pallas_call(
        _expert_kernel,
        out_shape=jax.ShapeDtypeStruct((rows, d), BF16),
        grid_spec=grid_spec,
        compiler_params=_cparams("arbitrary"),
        name="routed_experts",
    )(block_expert, n_used, xs, w_gate, w_up, w_down, buf_w.reshape(rows, 1))


def _take_rows(arr, idx):
    return arr.at[idx].get(mode="promise_in_bounds")


def _dispatch_plan(idx_t, gate_t, bm):
    k, t = idx_t.shape
    a = k * t
    experts = jnp.arange(N_EXPERTS, dtype=jnp.int32)
    flat_e = idx_t.T.reshape(a)
    gates = gate_t.T.reshape(a)
    iota = jnp.arange(a, dtype=jnp.int32)
    _, order = lax.sort_key_val(flat_e, iota)
    _, inv = lax.sort_key_val(order, iota)
    counts = jnp.sum((flat_e[:, None] == experts[None, :]).astype(jnp.int32), axis=0)
    starts = jnp.cumsum(counts) - counts
    padded = (counts + bm - 1) // bm * bm
    padded_end = jnp.cumsum(padded)
    padded_start = padded_end - padded
    nb = -(-a // bm) + N_EXPERTS
    n_used = (padded_end[-1:] // bm).astype(jnp.int32)
    blk = jnp.arange(nb, dtype=jnp.int32)
    block_expert = jnp.minimum(
        jnp.sum((blk[:, None] * bm >= padded_end[None, :]).astype(jnp.int32), axis=1), N_EXPERTS - 1)
    onehot = (block_expert[:, None] == experts[None, :]).astype(jnp.int32)
    b_pstart, b_count, b_start = [jnp.sum(onehot * v[None, :], axis=1)
                                  for v in (padded_start, counts, starts)]
    rank = blk[:, None] * bm + jnp.arange(bm, dtype=jnp.int32)[None, :] - b_pstart[:, None]
    valid = (rank < b_count[:, None]).reshape(nb * bm)
    src = jnp.clip(b_start[:, None] + rank, 0, a - 1).reshape(nb * bm)
    asg = _take_rows(order, src)
    buf_tok = jnp.where(valid, asg // k, 0)
    buf_w = jnp.where(valid, _take_rows(gates, asg), 0.0)
    shift = padded_start - starts
    pos = inv + _take_rows(shift, flat_e)
    return buf_tok, buf_w, pos, block_expert.astype(jnp.int32), n_used


def _swiglu_up_kernel(a_ref, wg_ref, wu_ref, o_ref):
    a = a_ref[...]
    g = jnp.dot(a, wg_ref[...], preferred_element_type=F32)
    u = jnp.dot(a, wu_ref[...], preferred_element_type=F32)
    o_ref[...] = (_silu(g) * u).astype(o_ref.dtype)


def _swiglu_up(a, wg, wu, layer):
    m, k = a.shape
    n = wg.shape[2]
    tm = _div_tile(m, MM_TM)
    tn = _div_tile(n, 512)
    wsp = pl.BlockSpec((None, k, tn), lambda i, j: (layer, 0, j))
    return pl.pallas_call(
        _swiglu_up_kernel,
        out_shape=jax.ShapeDtypeStruct((m, n), BF16),
        grid=(m // tm, n // tn),
        in_specs=[pl.BlockSpec((tm, k), lambda i, j: (i, 0)), wsp, wsp],
        out_specs=pl.BlockSpec((tm, tn), lambda i, j: (i, j)),
        compiler_params=_cparams("parallel", "arbitrary"),
        name="shared_swiglu_up",
    )(a, wg, wu)


def _lambda_init_for(layer):
    return 0.8 - 0.6 * math.exp(-0.3 * layer)


def kernel(x, c, ctx, c_ctx, w_ada, b_ada, norm_mix, norm_ffn, w_in, hgrn_lb, hgrn_norm, diff_lambda, diff_subln, w_branch, w_mgate, b_mgate, w_out, w_router, router_bias, w_exp_gate, w_exp_up, w_exp_down, w_sh_gate, w_sh_up, w_sh_down, norm_final):
    batch, seq, d = x.shape
    ctx_len = ctx.shape[1]
    depth = w_ada.shape[0]
    n_lat, n_ctx = batch * seq, batch * ctx_len
    n_all = n_lat + n_ctx
    assert batch < 8 and seq % GRID_W == 0 and ctx_len % 8 == 0

    p_lb = jax.nn.softmax(hgrn_lb.astype(F32), axis=1)
    lower_bounds = jnp.cumsum(p_lb, axis=1) - p_lb[:, :1]

    c8 = jnp.zeros((8, d), F32).at[:batch].set(c).at[batch].set(c_ctx)
    mod = _ada_mod(c8, w_ada, b_ada)

    rope_tm = _div_tile(math.gcd(seq, n_ctx), ROW_TILE)
    rope_tabs = _rope_tables(seq, rope_tm)
    hg_tabs = _hgrn_tables(_div_tile(math.gcd(seq, ctx_len), HG_CHUNK))
    rt_tabs = _ret_tables(_div_tile(math.gcd(seq, ctx_len), RT_CHUNK))

    w_in_b, w_mgate_b, w_branch_b, w_out_b = [w.astype(BF16) for w in (w_in, w_mgate, w_branch, w_out)]
    w_sg_b, w_su_b, w_sd_b = [w.astype(BF16) for w in (w_sh_gate, w_sh_up, w_sh_down)]

    xa = jnp.concatenate([x.reshape(n_lat, d), ctx.reshape(n_ctx, d)], axis=0)
    for l in range(depth):
        need_ctx = l < depth - 1
        rows = n_all if need_ctx else n_lat
        sh1, sc1, g1, sh2, sc2, g2 = [mod[l, :, i * d:(i + 1) * d].reshape(8, 1, d) for i in range(6)]

        h = _norm_mod(xa, norm_mix[l], sc1, sh1, n_all, seq, n_lat, batch, BF16)
        p = _matmul(h, w_in_b, l, BF16)
        qd, kd, qr, kr = _rope_prep(p, rope_tabs, rope_tm, seq, n_lat)
        y_hg, yc_hg = _hgrn(p, lower_bounds[0, l], lower_bounds[1, l], hgrn_norm[l], hg_tabs,
                            batch, seq, ctx_len, need_ctx)
        lam_init = _lambda_init_for(l)
        y_da = _diff_attn(qd, kd, p, diff_lambda[l], diff_subln[l], lam_init, batch, seq, ctx_len, True)
        y_rt, yc_rt = _retention(qr, kr, p, rt_tabs, batch, seq, ctx_len, need_ctx)
        ys = [y_hg, y_da, y_rt]
        if need_ctx:
            yc_da = _diff_attn(qd, kd, p, diff_lambda[l], diff_subln[l], lam_init, batch, seq,
                               ctx_len, False)
            ys = [jnp.concatenate([a, b], axis=0) for a, b in zip(ys, (yc_hg, yc_da, yc_rt))]
        merged = _merge(h, ys, w_mgate_b, b_mgate, w_branch_b, l, rows)
        xa = _proj_residual(merged, w_out_b, l, xa, g1, None, 0, rows, seq, n_lat, batch)

        h2, idx_t, gate_t = _norm_router(xa, norm_ffn[l], sc2, sh2, w_router[l], router_bias[l],
                                         rows, seq, n_lat, batch)
        buf_tok, buf_w, pos, block_expert, n_used = _dispatch_plan(idx_t, gate_t, MOE_BM)
        xs = _take_rows(h2, buf_tok)
        y_sorted = _experts(xs, w_exp_gate, w_exp_up, w_exp_down, l, buf_w, block_expert, n_used,
                            MOE_BM)
        y_slots = _take_rows(y_sorted, pos).reshape(rows, TOP_K * d)
        up = _swiglu_up(h2, w_sg_b, w_su_b, l)
        xa = _proj_residual(up, w_sd_b, l, xa, g2, y_slots, TOP_K, rows, seq, n_lat, batch)

    zeros = jnp.zeros((8, 1, d), F32)
    out = _norm_mod(xa, norm_final, zeros, zeros, n_lat, seq, n_lat, batch, F32)
    return out.reshape(batch, seq, d)
```

```python
import functools
import math

import numpy as np
import jax
import jax.numpy as jnp
from jax import lax
from jax.experimental import pallas as pl
from jax.experimental.pallas import tpu as pltpu

F32 = jnp.float32
BF16 = jnp.bfloat16

NORM_EPS = 1e-6
ROPE_BASE = 10000.0
GRID_W = 64
HG_HEADS = 8
DA_HEADS = 8
DA_DH = 64
RT_HEADS = 8
RT_DK = 64
N_EXPERTS = 64
TOP_K = 8
N_GROUPS = 8
TOPK_GROUPS = 4
EXPERT_FF = 256
ROUTED_SCALE = 2.5
BRANCH_W = 1024

LANES = 128
V7X_VMEM_BYTES = 64 * 1024 * 1024
VMEM_LIMIT = 56 * 1024 * 1024

COL_HG_Q, COL_HG_FF, COL_HG_FB, COL_HG_I, COL_HG_G = 0, 8, 16, 24, 32
COL_DA_Q, COL_DA_K, COL_DA_V = 40, 48, 56
COL_RT_Q, COL_RT_K, COL_RT_V, COL_RT_G = 64, 68, 72, 80

NT_DIMS = (((1,), (1,)), ((), ()))
TN_DIMS = (((0,), (0,)), ((), ()))

ROW_TILE = 256
MM_TM = 1024
HG_CHUNK = 128
RT_CHUNK = 256
ATT_TQ = 512
ATT_TK = 512
ATT_VROWS = LANES + 16
MOE_BM = 256
MOE_GROUPS = 2


def _cparams(*sem):
    return pltpu.CompilerParams(dimension_semantics=sem, vmem_limit_bytes=VMEM_LIMIT)


def _div_tile(n, pref):
    t = min(n, pref)
    while n % t:
        t //= 2
    return t


def _sigmoid(x):
    return 1.0 / (1.0 + jnp.exp(-x))


def _silu(x):
    return x * _sigmoid(x)


def _ada_kernel(c_ref, w_ref, b_ref, o_ref):
    a = _silu(c_ref[...]).astype(BF16)
    o_ref[...] = jnp.dot(a, w_ref[...].astype(BF16), preferred_element_type=F32) + b_ref[...]


def _ada_mod(c8, w_ada, b_ada):
    depth, d, n = w_ada.shape
    tn = _div_tile(n, 512)
    return pl.pallas_call(
        _ada_kernel,
        out_shape=jax.ShapeDtypeStruct((depth, 8, n), F32),
        grid=(depth, n // tn),
        in_specs=[
            pl.BlockSpec((8, d), lambda l, j: (0, 0)),
            pl.BlockSpec((None, d, tn), lambda l, j: (l, 0, j)),
            pl.BlockSpec((None, 1, tn), lambda l, j: (l, 0, j)),
        ],
        out_specs=pl.BlockSpec((None, 8, tn), lambda l, j: (l, 0, j)),
        compiler_params=_cparams("arbitrary", "arbitrary"),
        name="ada_mod",
    )(c8, w_ada, b_ada.reshape(depth, 1, n))


def _mod_row_map(tm, seq, n_lat, batch):
    def index_map(i):
        start = i * tm
        return (jnp.where(start < n_lat, start // seq, batch), 0, 0)
    return index_map


def _norm_mod_kernel(x_ref, w_ref, sc_ref, sh_ref, o_ref):
    x = x_ref[...]
    ms = jnp.mean(x * x, axis=-1, keepdims=True)
    y = x * lax.rsqrt(ms + NORM_EPS) * w_ref[...]
    o_ref[...] = (y * (1.0 + sc_ref[...]) + sh_ref[...]).astype(o_ref.dtype)


def _norm_mod(x, w, sc, sh, rows, seq, n_lat, batch, out_dtype):
    d = x.shape[1]
    tm = _div_tile(math.gcd(seq, rows), ROW_TILE)
    rmap = _mod_row_map(tm, seq, n_lat, batch)
    return pl.pallas_call(
        _norm_mod_kernel,
        out_shape=jax.ShapeDtypeStruct((rows, d), out_dtype),
        grid=(rows // tm,),
        in_specs=[
            pl.BlockSpec((tm, d), lambda i: (i, 0)),
            pl.BlockSpec((1, d), lambda i: (0, 0)),
            pl.BlockSpec((None, 1, d), rmap),
            pl.BlockSpec((None, 1, d), rmap),
        ],
        out_specs=pl.BlockSpec((tm, d), lambda i: (i, 0)),
        compiler_params=_cparams("parallel"),
        name="norm_mod",
    )(x, w.reshape(1, d), sc, sh)


def _mm_kernel(a_ref, b_ref, o_ref):
    o_ref[...] = jnp.dot(a_ref[...], b_ref[...], preferred_element_type=F32).astype(o_ref.dtype)


def _matmul(a, b, layer, out_dtype, tn_pref=1024):
    m, k = a.shape
    n = b.shape[2]
    tm = _div_tile(m, MM_TM)
    tn = _div_tile(n, tn_pref)
    return pl.pallas_call(
        _mm_kernel,
        out_shape=jax.ShapeDtypeStruct((m, n), out_dtype),
        grid=(m // tm, n // tn),
        in_specs=[pl.BlockSpec((tm, k), lambda i, j: (i, 0)),
                  pl.BlockSpec((None, k, tn), lambda i, j: (layer, 0, j))],
        out_specs=pl.BlockSpec((tm, tn), lambda i, j: (i, j)),
        compiler_params=_cparams("parallel", "arbitrary"),
        name="matmul",
    )(a, b)


def _rope_tables(seq, n_ctx):
    pos = np.arange(seq, dtype=np.float64)
    rows = np.floor(pos / GRID_W)
    cols = pos - rows * GRID_W
    lane = np.arange(LANES)
    j = lane % 64
    part, jj = j // 32, j % 32
    inv = ROPE_BASE ** (-np.arange(0, 32, 2, dtype=np.float64) / 32.0)
    ang = np.where(part[None, :] == 0, rows[:, None], cols[:, None]) * inv[jj % 16][None, :]
    cos_da = np.cos(ang)
    sin_da = np.sin(ang) * np.where(jj < 16, -1.0, 1.0)[None, :]
    partner_da = np.where(jj < 16, lane + 16, lane - 16)
    theta = 1.0 / (ROPE_BASE ** np.linspace(0.0, 1.0, 32))
    ang = pos[:, None] * theta[j % 32][None, :]
    cos_rt = np.cos(ang)
    sin_rt = np.sin(ang) * np.where(j < 32, -1.0, 1.0)[None, :]
    partner_rt = np.where(j < 32, lane + 32, lane - 32)

    def full(tab, fill):
        return jnp.asarray(np.concatenate([tab, np.full((n_ctx, LANES), fill)], 0), F32)

    def perm(partner):
        p = np.zeros((LANES, LANES), np.float32)
        p[partner, lane] = 1.0
        return jnp.asarray(p, BF16)

    return (full(cos_da, 1.0), full(sin_da, 0.0), full(cos_rt, 1.0), full(sin_rt, 0.0),
            perm(partner_da), perm(partner_rt))


def _rope_kernel(qd_ref, kd_ref, qr_ref, kr_ref, cd_ref, sd_ref, cr_ref, sr_ref, pd_ref, pr_ref,
                 oqd_ref, okd_ref, oqr_ref, okr_ref):
    def rot(src, dst, cos_ref, sin_ref, perm_ref, scale):
        cos = cos_ref[...]
        sin = sin_ref[...]
        perm = perm_ref[...]
        for g in range(src.shape[1] // LANES):
            sl = slice(g * LANES, (g + 1) * LANES)
            x = src[:, sl]
            xs = jnp.dot(x, perm, preferred_element_type=F32)
            y = x.astype(F32) * cos + xs * sin
            dst[:, sl] = (y * scale).astype(dst.dtype)

    rot(qd_ref, oqd_ref, cd_ref, sd_ref, pd_ref, DA_DH ** -0.5 * math.log2(math.e))
    rot(kd_ref, okd_ref, cd_ref, sd_ref, pd_ref, 1.0)
    rot(qr_ref, oqr_ref, cr_ref, sr_ref, pr_ref, 1.0)
    rot(kr_ref, okr_ref, cr_ref, sr_ref, pr_ref, RT_DK ** -0.5)


def _rope_prep(p, tables, tm, seq, n_lat):
    rows = p.shape[0]
    cos_da, sin_da, cos_rt, sin_rt, perm_da, perm_rt = tables
    lat_tiles, seq_tiles = n_lat // tm, seq // tm

    def tmap(i):
        return (jnp.where(i < lat_tiles, i % seq_tiles, seq_tiles), 0)

    wd, wr = DA_HEADS * 2 * DA_DH, RT_HEADS * RT_DK
    tab = pl.BlockSpec((tm, LANES), tmap)
    pm = pl.BlockSpec((LANES, LANES), lambda i: (0, 0))
    return pl.pallas_call(
        _rope_kernel,
        out_shape=(jax.ShapeDtypeStruct((rows, wd), BF16), jax.ShapeDtypeStruct((rows, wd), BF16),
                   jax.ShapeDtypeStruct((rows, wr), BF16), jax.ShapeDtypeStruct((rows, wr), BF16)),
        grid=(rows // tm,),
        in_specs=[
            pl.BlockSpec((tm, wd), lambda i: (i, COL_DA_Q * LANES // wd)),
            pl.BlockSpec((tm, wd), lambda i: (i, COL_DA_K * LANES // wd)),
            pl.BlockSpec((tm, wr), lambda i: (i, COL_RT_Q * LANES // wr)),
            pl.BlockSpec((tm, wr), lambda i: (i, COL_RT_K * LANES // wr)),
            tab, tab, tab, tab, pm, pm,
        ],
        out_specs=(pl.BlockSpec((tm, wd), lambda i: (i, 0)), pl.BlockSpec((tm, wd), lambda i: (i, 0)),
                   pl.BlockSpec((tm, wr), lambda i: (i, 0)), pl.BlockSpec((tm, wr), lambda i: (i, 0))),
        compiler_params=_cparams("parallel"),
        name="rope_prep",
    )(p, p, p, p, cos_da, sin_da, cos_rt, sin_rt, perm_da, perm_rt)


def _hgrn_tables(c):
    nlev = int(math.log2(c))
    m = np.zeros((nlev + 2, c, c), np.float32)
    w = np.zeros((nlev + 1, c, c), np.float32)
    for l in range(nlev):
        bs = c >> (l + 1)
        for t in range(c):
            blk = t // bs
            if blk % 2 == 1:
                m[l, t, blk * bs:t + 1] = 1.0
                w[l, t, (blk - 1) * bs:blk * bs] = 1.0
            else:
                m[l, t, t + 1:(blk + 1) * bs] = 1.0
    for t in range(c):
        m[nlev, t, :t + 1] = 1.0
        m[nlev + 1, t, t + 1:] = 1.0
    w[nlev] = np.eye(c)
    mb = m[:, ::-1, ::-1]
    wb = w[:, ::-1, ::-1]
    r = (nlev + 2) * c
    return (jnp.asarray(m.reshape(r, c), BF16), jnp.asarray(mb.reshape(r, c), BF16),
            jnp.asarray(w, F32), jnp.asarray(wb, F32))


def _hgrn_chunk(q_raw, f_raw, v, lb, m_ref, w_ref, st, fwd):
    c = q_raw.shape[0]
    nlev = w_ref.shape[0] - 1
    q = _silu(q_raw.astype(F32))
    f = lb + (1.0 - lb) * _sigmoid(f_raw.astype(F32))
    lf = jnp.log(f)
    k = 1.0 - f
    hi = lf.astype(BF16)
    lo = (lf - hi.astype(F32)).astype(BF16)
    g2 = jnp.dot(m_ref[...], jnp.concatenate([hi, lo], axis=1), preferred_element_type=F32)
    g = g2[:, :LANES] + g2[:, LANES:]
    e = jnp.exp(g)
    row = lax.broadcasted_iota(jnp.int32, (c, LANES), 0)
    a = w_ref[nlev] * lax.dot_general(q.astype(BF16), k.astype(BF16), NT_DIMS,
                                      preferred_element_type=F32)
    for l in range(nlev):
        shift = int(math.log2(c >> (l + 1)))
        odd = ((row >> shift) & 1) == 1
        x = (jnp.where(odd, q, k) if fwd else jnp.where(odd, k, q)) * e[l * c:(l + 1) * c]
        xb = x.astype(BF16)
        a = a + w_ref[l] * lax.dot_general(xb, xb, NT_DIMS, preferred_element_type=F32)
    qe = (q * e[nlev * c:(nlev + 1) * c]).astype(BF16)
    ke = (k * e[(nlev + 1) * c:(nlev + 2) * c]).astype(BF16)
    o = (jnp.dot(a.astype(BF16), v, preferred_element_type=F32)
         + lax.dot_general(qe, st.astype(BF16), NT_DIMS, preferred_element_type=F32))
    last = nlev * c + (c - 1 if fwd else 0)
    dec = jnp.exp(g[last:last + 1, :])
    st_new = dec * st + lax.dot_general(v, ke, TN_DIMS, preferred_element_type=F32)
    return o, st_new


def _hgrn_kernel(need_ctx, chunk, *refs):
    (ql, ffl, fbl, il, gl, qc, ffc, fbc, ic, gc, lbf_ref, lbb_ref, nw_ref,
     mf_ref, mb_ref, wf_ref, wb_ref) = refs[:17]
    rest = refs[17:]
    if need_ctx:
        yl_ref, yc_ref, of_ref, ob_ref, ocf_ref, ocb_ref, st_ref = rest
    else:
        yl_ref, of_ref, ob_ref, st_ref = rest
        yc_ref = ocf_ref = ocb_ref = None
    nc_lat = ql.shape[0] // chunk
    nc_ctx = qc.shape[0] // chunk
    lbf = lbf_ref[...]
    lbb = lbb_ref[...]
    st_ref[...] = jnp.zeros_like(st_ref)

    def step(j, q_ref, ff_ref, fb_ref, i_ref, n, outf, outb):
        sf = pl.ds(pl.multiple_of(j * chunk, chunk), chunk)
        sb = pl.ds(pl.multiple_of((n - 1 - j) * chunk, chunk), chunk)
        o, s = _hgrn_chunk(q_ref[sf, :], ff_ref[sf, :], i_ref[sf, :], lbf, mf_ref, wf_ref,
                           st_ref[0], True)
        st_ref[0] = s
        if outf is not None:
            outf[sf, :] = o
        o, s = _hgrn_chunk(q_ref[sb, :], fb_ref[sb, :], i_ref[sb, :], lbb, mb_ref, wb_ref,
                           st_ref[1], False)
        st_ref[1] = s
        if outb is not None:
            outb[sb, :] = o

    def ctx_body(j, carry):
        step(j, qc, ffc, fbc, ic, nc_ctx, ocf_ref, ocb_ref)
        return carry

    def lat_body(j, carry):
        step(j, ql, ffl, fbl, il, nc_lat, of_ref, ob_ref)
        return carry

    lax.fori_loop(0, nc_ctx, ctx_body, 0)
    lax.fori_loop(0, nc_lat, lat_body, 0, unroll=2)

    nw = nw_ref[...]

    def readout(n, a_ref, b_ref, g_ref, y_ref):
        def body(j, carry):
            s = pl.ds(pl.multiple_of(j * chunk, chunk), chunk)
            o = a_ref[s, :] + b_ref[s, :]
            o = o * lax.rsqrt(jnp.mean(o * o, axis=-1, keepdims=True) + NORM_EPS) * nw
            y_ref[s, :] = (o * _silu(g_ref[s, :].astype(F32))).astype(y_ref.dtype)
            return carry
        lax.fori_loop(0, n, body, 0)

    readout(nc_lat, of_ref, ob_ref, gl, yl_ref)
    if need_ctx:
        readout(nc_ctx, ocf_ref, ocb_ref, gc, yc_ref)


def _hgrn(p, lb_f, lb_b, norm_w, tables, batch, seq, ctx_len, need_ctx):
    n_lat = batch * seq
    chunk = _div_tile(math.gcd(seq, ctx_len), HG_CHUNK)
    mf, mb, wf, wb = tables
    ctx_blk0 = n_lat // ctx_len
    width = HG_HEADS * LANES

    def lat(col):
        return pl.BlockSpec((seq, LANES), lambda b, h: (b, col + h))

    def ctx(col):
        return pl.BlockSpec((ctx_len, LANES), lambda b, h: (ctx_blk0 + b, col + h))

    head_vec = pl.BlockSpec((1, LANES), lambda b, h: (0, h))
    const2 = lambda arr: pl.BlockSpec(arr.shape, lambda b, h: (0,) * arr.ndim)
    cols = (COL_HG_Q, COL_HG_FF, COL_HG_FB, COL_HG_I, COL_HG_G)
    in_specs = ([lat(c) for c in cols] + [ctx(c) for c in cols]
                + [head_vec, head_vec, pl.BlockSpec((1, LANES), lambda b, h: (0, 0)),
                   const2(mf), const2(mb), const2(wf), const2(wb)])
    out_shape = [jax.ShapeDtypeStruct((n_lat, width), BF16)]
    out_specs = [pl.BlockSpec((seq, LANES), lambda b, h: (b, h))]
    scratch = [pltpu.VMEM((seq, LANES), F32), pltpu.VMEM((seq, LANES), F32)]
    if need_ctx:
        out_shape.append(jax.ShapeDtypeStruct((batch * ctx_len, width), BF16))
        out_specs.append(pl.BlockSpec((ctx_len, LANES), lambda b, h: (b, h)))
        scratch += [pltpu.VMEM((ctx_len, LANES), F32), pltpu.VMEM((ctx_len, LANES), F32)]
    scratch.append(pltpu.VMEM((2, LANES, LANES), F32))
    outs = pl.pallas_call(
        functools.partial(_hgrn_kernel, need_ctx, chunk),
        out_shape=tuple(out_shape),
        grid=(batch, HG_HEADS),
        in_specs=in_specs,
        out_specs=tuple(out_specs),
        scratch_shapes=scratch,
        compiler_params=_cparams("parallel", "parallel"),
        name="hgrn2_scan",
    )(*([p] * 10), lb_f.reshape(1, width), lb_b.reshape(1, width), norm_w.reshape(1, LANES),
      mf, mb, wf, wb)
    return outs[0], (outs[1] if need_ctx else None)


def _attn_kernel(has_lat, tk, lambda_init, *refs):
    if has_lat:
        (q_ref, kc_ref, vc_ref, kl_ref, vl_ref, lam_ref, sub_ref, o_ref,
         s_ref, vtc_ref, vtl_ref) = refs
    else:
        q_ref, kc_ref, vc_ref, lam_ref, sub_ref, o_ref, s_ref, vtc_ref = refs
    tq = q_ref.shape[0]
    w2 = 2 * tq
    nctx = kc_ref.shape[0]
    nlat = kl_ref.shape[0] if has_lat else 0
    n_chunks = nlat // tk
    sub_t = min(256, tk)

    @pl.when(pl.program_id(2) == 0)
    def _():
        vtc_ref[LANES:, :] = jnp.ones((ATT_VROWS - LANES, nctx), BF16)
        for r in range(0, nctx, min(sub_t, nctx)):
            rr = slice(r, r + min(sub_t, nctx))
            vtc_ref[:LANES, rr] = vc_ref[rr, :].astype(F32).T.astype(BF16)
        for j in range(n_chunks):
            vtl_ref[j, LANES:, :] = jnp.ones((ATT_VROWS - LANES, tk), BF16)
            for r in range(0, tk, sub_t):
                vtl_ref[j, :LANES, r:r + sub_t] = (
                    vl_ref[j * tk + r:j * tk + r + sub_t, :].astype(F32).T.astype(BF16))

    q = q_ref[...].astype(F32).T
    sub = lax.broadcasted_iota(jnp.int32, (LANES, tq), 0)
    qst = jnp.concatenate([jnp.where(sub < DA_DH, q, 0.0), jnp.where(sub >= DA_DH, q, 0.0)],
                          axis=1).astype(BF16)

    def col_fold(x, op):
        return op(x.reshape(x.shape[0] // 8, 8, w2), axis=0)

    def scores(k_ref, src, dst, macc):
        s = jnp.dot(k_ref[src, :], qst, preferred_element_type=F32)
        s_ref[dst, :] = s
        return jnp.maximum(macc, col_fold(s, jnp.max))

    macc = scores(kc_ref, slice(0, nctx), slice(0, nctx), jnp.full((8, w2), -jnp.inf, F32))
    for j in range(n_chunks):
        macc = scores(kl_ref, slice(j * tk, (j + 1) * tk), slice(nctx + j * tk, nctx + (j + 1) * tk),
                      macc)
    m = jnp.max(macc, axis=0, keepdims=True)

    def weights(rows):
        return jnp.exp2((s_ref[rows, :] - m).astype(BF16))

    pv = jnp.dot(vtc_ref[...], weights(slice(0, nctx)), preferred_element_type=F32)
    for j in range(n_chunks):
        pv = pv + jnp.dot(vtl_ref[j], weights(slice(nctx + j * tk, nctx + (j + 1) * tk)),
                          preferred_element_type=F32)
    on = pv[:LANES] / pv[LANES:LANES + 1]
    lv = lam_ref[...]
    lam = (jnp.exp(jnp.sum(lv[0:1] * lv[1:2], axis=-1, keepdims=True))
           - jnp.exp(jnp.sum(lv[2:3] * lv[3:4], axis=-1, keepdims=True)) + lambda_init)
    o = on[:, :tq] - lam * on[:, tq:]
    o = o * lax.rsqrt(jnp.mean(o * o, axis=0, keepdims=True) + NORM_EPS) * sub_ref[...]
    o_ref[...] = (o * (1.0 - lambda_init)).T.astype(o_ref.dtype)


def _diff_attn(qd, kd, p, lam_vec, subln_w, lambda_init, batch, seq, ctx_len, latent_queries):
    n_lat = batch * seq
    ctx_blk0 = n_lat // ctx_len
    width = DA_HEADS * LANES
    lq = seq if latent_queries else ctx_len
    tq = _div_tile(lq, ATT_TQ)
    q_blk0 = 0 if latent_queries else n_lat // tq
    qt = lq // tq
    tk = _div_tile(seq, ATT_TK)
    in_specs = [
        pl.BlockSpec((tq, LANES), lambda b, h, i: (q_blk0 + b * qt + i, h)),
        pl.BlockSpec((ctx_len, LANES), lambda b, h, i: (ctx_blk0 + b, h)),
        pl.BlockSpec((ctx_len, LANES), lambda b, h, i: (ctx_blk0 + b, COL_DA_V + h)),
    ]
    args = [qd, kd, p]
    n_keys = ctx_len + (seq if latent_queries else 0)
    scratch = [pltpu.VMEM((n_keys, 2 * tq), F32), pltpu.VMEM((ATT_VROWS, ctx_len), BF16)]
    if latent_queries:
        in_specs += [pl.BlockSpec((seq, LANES), lambda b, h, i: (b, h)),
                     pl.BlockSpec((seq, LANES), lambda b, h, i: (b, COL_DA_V + h))]
        args += [kd, p]
        scratch.append(pltpu.VMEM((seq // tk, ATT_VROWS, tk), BF16))
    in_specs += [pl.BlockSpec((4, DA_DH), lambda b, h, i: (0, 0)),
                 pl.BlockSpec((LANES, 1), lambda b, h, i: (0, 0))]
    args += [lam_vec, subln_w.reshape(LANES, 1)]
    return pl.pallas_call(
        functools.partial(_attn_kernel, latent_queries, tk, lambda_init),
        out_shape=jax.ShapeDtypeStruct((batch * lq, width), BF16),
        grid=(batch, DA_HEADS, qt),
        in_specs=in_specs,
        out_specs=pl.BlockSpec((tq, LANES), lambda b, h, i: (b * qt + i, h)),
        scratch_shapes=scratch,
        compiler_params=_cparams("parallel", "parallel", "arbitrary"),
        name="diff_attn_lat" if latent_queries else "diff_attn_ctx",
    )(*args)


def _ret_tables(c):
    gam = 1.0 - 2.0 ** (-5.0 - np.arange(RT_HEADS, dtype=np.float64))
    idx = np.arange(c, dtype=np.float64)
    dist = np.abs(idx[:, None] - idx[None, :])
    dsym = gam[:, None, None] ** dist[None] * np.where(dist == 0, 2.0, 1.0)[None]
    lane_head = np.arange(LANES) // RT_DK
    mask = (lane_head[None, :] == (np.arange(RT_HEADS) % 2)[:, None]).astype(np.float64)

    def tab(power):
        return (gam[:, None] ** power[None, :])[:, :, None] * mask[:, None, :]

    qdf, kdf = tab(idx + 1.0), tab(c - 1.0 - idx)
    qdb, kdb = tab(c - idx), tab(idx)
    cdec = np.broadcast_to((gam ** c)[:, None, None], (RT_HEADS, 1, LANES))
    hp = RT_HEADS // 2
    f = lambda x: jnp.asarray(np.ascontiguousarray(x).reshape((hp, 2) + x.shape[1:]), F32)
    return f(dsym), f(mask[:, None, :]), f(qdf), f(kdf), f(qdb), f(kdb), f(cdec)


def _ret_kernel(need_ctx, chunk, *refs):
    (ql, kl, vl, gl, qc, kc, vc, gc, dsym_ref, msk_ref, qdf_ref, kdf_ref, qdb_ref, kdb_ref,
     cdec_ref) = refs[:15]
    rest = refs[15:]
    if need_ctx:
        yl_ref, yc_ref, of_ref, ob_ref, ocf_ref, ocb_ref, st_ref = rest
    else:
        yl_ref, of_ref, ob_ref, st_ref = rest
        yc_ref = ocf_ref = ocb_ref = None
    nc_lat = ql.shape[0] // chunk
    nc_ctx = qc.shape[0] // chunk
    st_ref[...] = jnp.zeros_like(st_ref)

    def step(j, q_ref, k_ref, v_ref, n, outf, outb):
        sf = pl.ds(pl.multiple_of(j * chunk, chunk), chunk)
        sb = pl.ds(pl.multiple_of((n - 1 - j) * chunk, chunk), chunk)
        qf = q_ref[sf, :].astype(F32)
        kfb = k_ref[sf, :]
        kf = kfb.astype(F32)
        qb = q_ref[sb, :].astype(F32)
        kb = k_ref[sb, :].astype(F32)
        for h in range(2):
            hs = slice(h * LANES, (h + 1) * LANES)
            vf = v_ref[sf, hs]
            vb = v_ref[sb, hs]
            a = lax.dot_general((qf * msk_ref[h]).astype(BF16), kfb, NT_DIMS,
                                preferred_element_type=F32) * dsym_ref[h]
            o = (jnp.dot(a.astype(BF16), vf, preferred_element_type=F32)
                 + jnp.dot((qf * qdf_ref[h]).astype(BF16), st_ref[h].astype(BF16),
                           preferred_element_type=F32))
            st_ref[h] = cdec_ref[h] * st_ref[h] + lax.dot_general(
                (kf * kdf_ref[h]).astype(BF16), vf, TN_DIMS, preferred_element_type=F32)
            if outf is not None:
                outf[sf, hs] = o
            o = jnp.dot((qb * qdb_ref[h]).astype(BF16), st_ref[2 + h].astype(BF16),
                        preferred_element_type=F32)
            st_ref[2 + h] = cdec_ref[h] * st_ref[2 + h] + lax.dot_general(
                (kb * kdb_ref[h]).astype(BF16), vb, TN_DIMS, preferred_element_type=F32)
            if outb is not None:
                outb[sb, hs] = o

    def ctx_body(j, carry):
        step(j, qc, kc, vc, nc_ctx, ocf_ref, ocb_ref)
        return carry

    def lat_body(j, carry):
        step(j, ql, kl, vl, nc_lat, of_ref, ob_ref)
        return carry

    lax.fori_loop(0, nc_ctx, ctx_body, 0)
    lax.fori_loop(0, nc_lat, lat_body, 0)

    def readout(n, a_ref, b_ref, g_ref, y_ref):
        def body(j, carry):
            s = pl.ds(pl.multiple_of(j * chunk, chunk), chunk)
            for h in range(2):
                hs = slice(h * LANES, (h + 1) * LANES)
                o = a_ref[s, hs] + b_ref[s, hs]
                o = o * lax.rsqrt(jnp.mean(o * o, axis=-1, keepdims=True) + NORM_EPS)
                y_ref[s, hs] = (o * _silu(g_ref[s, hs].astype(F32))).astype(y_ref.dtype)
            return carry
        lax.fori_loop(0, n, body, 0)

    readout(nc_lat, of_ref, ob_ref, gl, yl_ref)
    if need_ctx:
        readout(nc_ctx, ocf_ref, ocb_ref, gc, yc_ref)


def _retention(qr, kr, p, tables, batch, seq, ctx_len, need_ctx):
    n_lat = batch * seq
    chunk = _div_tile(math.gcd(seq, ctx_len), RT_CHUNK)
    ctx_blk0 = n_lat // ctx_len
    hp = RT_HEADS // 2
    pair = 2 * LANES
    width = RT_HEADS * LANES
    v_col, g_col = COL_RT_V * LANES // pair, COL_RT_G * LANES // pair

    in_specs = [
        pl.BlockSpec((seq, LANES), lambda b, h: (b, h)),
        pl.BlockSpec((seq, LANES), lambda b, h: (b, h)),
        pl.BlockSpec((seq, pair), lambda b, h: (b, v_col + h)),
        pl.BlockSpec((seq, pair), lambda b, h: (b, g_col + h)),
        pl.BlockSpec((ctx_len, LANES), lambda b, h: (ctx_blk0 + b, h)),
        pl.BlockSpec((ctx_len, LANES), lambda b, h: (ctx_blk0 + b, h)),
        pl.BlockSpec((ctx_len, pair), lambda b, h: (ctx_blk0 + b, v_col + h)),
        pl.BlockSpec((ctx_len, pair), lambda b, h: (ctx_blk0 + b, g_col + h)),
    ]
    for t in tables:
        in_specs.append(pl.BlockSpec((None,) + t.shape[1:], lambda b, h: (h, 0, 0, 0)))
    out_shape = [jax.ShapeDtypeStruct((n_lat, width), BF16)]
    out_specs = [pl.BlockSpec((seq, pair), lambda b, h: (b, h))]
    scratch = [pltpu.VMEM((seq, pair), F32), pltpu.VMEM((seq, pair), F32)]
    if need_ctx:
        out_shape.append(jax.ShapeDtypeStruct((batch * ctx_len, width), BF16))
        out_specs.append(pl.BlockSpec((ctx_len, pair), lambda b, h: (b, h)))
        scratch += [pltpu.VMEM((ctx_len, pair), F32), pltpu.VMEM((ctx_len, pair), F32)]
    scratch.append(pltpu.VMEM((4, LANES, LANES), F32))
    outs = pl.pallas_call(
        functools.partial(_ret_kernel, need_ctx, chunk),
        out_shape=tuple(out_shape),
        grid=(batch, hp),
        in_specs=in_specs,
        out_specs=tuple(out_specs),
        scratch_shapes=scratch,
        compiler_params=_cparams("parallel", "parallel"),
        name="retention_scan",
    )(qr, kr, p, p, qr, kr, p, p, *tables)
    return outs[0], (outs[1] if need_ctx else None)


def _merge_kernel(h_ref, y0_ref, y1_ref, y2_ref, wg_ref, bg_ref, wb_ref, o_ref):
    h = h_ref[...]
    acc = None
    for i, y_ref in enumerate((y0_ref, y1_ref, y2_ref)):
        gate = _sigmoid(jnp.dot(h, wg_ref[i], preferred_element_type=F32) + bg_ref[i])
        t = gate * jnp.dot(y_ref[...], wb_ref[i], preferred_element_type=F32)
        acc = t if acc is None else acc + t
    o_ref[...] = acc.astype(o_ref.dtype)


def _merge(h, ys, w_mgate, b_mgate, w_branch, layer, rows):
    d = h.shape[1]
    tm = _div_tile(rows, MM_TM)
    tn = _div_tile(d, 256)
    ysp = pl.BlockSpec((tm, BRANCH_W), lambda i, j: (i, 0))
    return pl.pallas_call(
        _merge_kernel,
        out_shape=jax.ShapeDtypeStruct((rows, d), BF16),
        grid=(rows // tm, d // tn),
        in_specs=[pl.BlockSpec((tm, d), lambda i, j: (i, 0)), ysp, ysp, ysp,
                  pl.BlockSpec((None, 3, d, tn), lambda i, j: (layer, 0, 0, j)),
                  pl.BlockSpec((None, 3, 1, tn), lambda i, j: (layer, 0, 0, j)),
                  pl.BlockSpec((None, 3, BRANCH_W, tn), lambda i, j: (layer, 0, 0, j))],
        out_specs=pl.BlockSpec((tm, tn), lambda i, j: (i, j)),
        compiler_params=_cparams("parallel", "arbitrary"),
        name="branch_merge",
    )(h, *ys, w_mgate, b_mgate.reshape(b_mgate.shape[0], 3, 1, d), w_branch)


def _proj_res_kernel(has_extra, *refs):
    if has_extra:
        a_ref, w_ref, x_ref, g_ref, e_ref, o_ref = refs
    else:
        a_ref, w_ref, x_ref, g_ref, o_ref = refs
    y = jnp.dot(a_ref[...], w_ref[...], preferred_element_type=F32)
    if has_extra:
        for s in range(e_ref.shape[0]):
            y = y + e_ref[s].astype(F32)
    o_ref[...] = x_ref[...] + g_ref[...] * y


def _proj_tile(rows, seq):
    return _div_tile(math.gcd(seq, rows), MM_TM)


def _proj_residual(a, w, layer, x, gate, extra, tile0, n_tiles, tm, seq, n_lat, batch):
    k = a.shape[1]
    d = w.shape[2]
    tn = _div_tile(d, 512)
    rows = n_tiles * tm
    rmap = _mod_row_map(tm, seq, n_lat, batch)
    in_specs = [pl.BlockSpec((tm, k), lambda i, j: (tile0 + i, 0)),
                pl.BlockSpec((None, k, tn), lambda i, j: (layer, 0, j)),
                pl.BlockSpec((tm, tn), lambda i, j: (tile0 + i, j)),
                pl.BlockSpec((None, 1, tn), lambda i, j: rmap(tile0 + i)[:2] + (j,))]
    args = [a, w, x, gate]
    n_slots = 0
    if extra is not None:
        n_slots = extra.shape[0]
        in_specs.append(pl.BlockSpec((n_slots, tm, tn), lambda i, j: (0, i, j)))
        args.append(extra)
    cost = pl.CostEstimate(flops=2 * rows * k * d, transcendentals=0,
                           bytes_accessed=rows * d * (8 + 2 * n_slots) + 2 * rows * k
                           + 2 * k * d * n_tiles)
    return pl.pallas_call(
        functools.partial(_proj_res_kernel, extra is not None),
        out_shape=jax.ShapeDtypeStruct((rows, d), F32),
        grid=(n_tiles, d // tn),
        in_specs=in_specs,
        out_specs=pl.BlockSpec((tm, tn), lambda i, j: (i, j)),
        compiler_params=_cparams("parallel", "arbitrary"),
        cost_estimate=cost,
        name="proj_residual",
    )(*args)


def _router_kernel(x_ref, w_ref, sc_ref, sh_ref, wr_ref, rb_ref, h_ref, idx_ref, gate_ref):
    x = x_ref[...]
    ms = jnp.mean(x * x, axis=-1, keepdims=True)
    h = x * lax.rsqrt(ms + NORM_EPS) * w_ref[...] * (1.0 + sc_ref[...]) + sh_ref[...]
    h_ref[...] = h.astype(h_ref.dtype)
    tm = x.shape[0]
    logits = lax.dot_general(wr_ref[...], h, NT_DIMS, precision=lax.Precision.HIGHEST,
                             preferred_element_type=F32)
    scores = _sigmoid(logits)
    choice = scores + rb_ref[...]
    per_group = N_EXPERTS // N_GROUPS
    neg = -jnp.inf
    sub = lax.broadcasted_iota(jnp.int32, (per_group, tm), 0).astype(F32)
    gs = []
    for g in range(N_GROUPS):
        xg = choice[g * per_group:(g + 1) * per_group, :]
        m1 = jnp.max(xg, axis=0, keepdims=True)
        i1 = jnp.min(jnp.where(xg == m1, sub, float(per_group)), axis=0, keepdims=True)
        m2 = jnp.max(jnp.where(sub == i1, neg, xg), axis=0, keepdims=True)
        gs.append(m1 + m2)
    sel = [jnp.zeros((1, tm), F32) for _ in range(N_GROUPS)]
    for _ in range(TOPK_GROUPS):
        m = functools.reduce(jnp.maximum, gs)
        found = jnp.zeros((1, tm), F32)
        for g in range(N_GROUPS):
            hit = jnp.where(gs[g] == m, 1.0 - found, 0.0)
            sel[g] = sel[g] + hit
            found = found + hit
            gs[g] = jnp.where(hit > 0.5, neg, gs[g])
    masked = jnp.concatenate(
        [jnp.where(sel[g] > 0.5, choice[g * per_group:(g + 1) * per_group, :], neg)
         for g in range(N_GROUPS)], axis=0)
    eidx = lax.broadcasted_iota(jnp.int32, (N_EXPERTS, tm), 0).astype(F32)
    idxs, ws = [], []
    for _ in range(TOP_K):
        m = jnp.max(masked, axis=0, keepdims=True)
        i = jnp.min(jnp.where(masked == m, eidx, float(N_EXPERTS)), axis=0, keepdims=True)
        hit = eidx == i
        ws.append(jnp.sum(jnp.where(hit, scores, 0.0), axis=0, keepdims=True))
        idxs.append(i)
        masked = jnp.where(hit, neg, masked)
    wsum = functools.reduce(lambda a, b: a + b, ws)
    idx_ref[...] = jnp.concatenate(idxs, axis=0).astype(jnp.int32)
    gate_ref[...] = jnp.concatenate(ws, axis=0) / wsum * ROUTED_SCALE


def _norm_router(x, w, sc, sh, w_router, router_bias, rows, seq, n_lat, batch):
    d = x.shape[1]
    tm = _div_tile(math.gcd(seq, rows), ROW_TILE)
    rmap = _mod_row_map(tm, seq, n_lat, batch)
    return pl.pallas_call(
        _router_kernel,
        out_shape=(jax.ShapeDtypeStruct((rows, d), BF16),
                   jax.ShapeDtypeStruct((TOP_K, rows), jnp.int32),
                   jax.ShapeDtypeStruct((TOP_K, rows), F32)),
        grid=(rows // tm,),
        in_specs=[pl.BlockSpec((tm, d), lambda i: (i, 0)),
                  pl.BlockSpec((1, d), lambda i: (0, 0)),
                  pl.BlockSpec((None, 1, d), rmap),
                  pl.BlockSpec((None, 1, d), rmap),
                  pl.BlockSpec((N_EXPERTS, d), lambda i: (0, 0)),
                  pl.BlockSpec((N_EXPERTS, 1), lambda i: (0, 0))],
        out_specs=(pl.BlockSpec((tm, d), lambda i: (i, 0)),
                   pl.BlockSpec((TOP_K, tm), lambda i: (0, i)),
                   pl.BlockSpec((TOP_K, tm), lambda i: (0, i))),
        compiler_params=_cparams("parallel"),
        name="norm_router",
    )(x, w.reshape(1, d), sc, sh, w_router.T, router_bias.reshape(N_EXPERTS, 1))


def _expert_kernel(be_ref, nu_ref, x_ref, wg_ref, wu_ref, wd_ref, wt_ref, o_ref,
                   wgb_ref, wub_ref, wdb_ref):
    i = pl.program_id(0)

    @pl.when((i == 0) | (be_ref[i] != be_ref[jnp.maximum(i - 1, 0)]))
    def _():
        wgb_ref[...] = wg_ref[...].astype(BF16)
        wub_ref[...] = wu_ref[...].astype(BF16)
        wdb_ref[...] = wd_ref[...].astype(BF16)

    @pl.when(i < nu_ref[0])
    def _():
        x = x_ref[...]
        g = jnp.dot(x, wgb_ref[...], preferred_element_type=F32)
        u = jnp.dot(x, wub_ref[...], preferred_element_type=F32)
        act = (_silu(g) * u).astype(BF16)
        y = jnp.dot(act, wdb_ref[...], preferred_element_type=F32)
        o_ref[...] = (y * wt_ref[...]).astype(o_ref.dtype)

    @pl.when(i >= nu_ref[0])
    def _():
        o_ref[...] = jnp.zeros_like(o_ref)


def _experts(xs, w_gate, w_up, w_down, layer, buf_w, block_expert, n_used, bm):
    rows, d = xs.shape
    nb = rows // bm
    ff = w_gate.shape[3]
    grid_spec = pltpu.PrefetchScalarGridSpec(
        num_scalar_prefetch=2,
        grid=(nb,),
        in_specs=[pl.BlockSpec((bm, d), lambda i, be, nu: (i, 0)),
                  pl.BlockSpec((None, None, d, ff), lambda i, be, nu: (layer, be[i], 0, 0)),
                  pl.BlockSpec((None, None, d, ff), lambda i, be, nu: (layer, be[i], 0, 0)),
                  pl.BlockSpec((None, None, ff, d), lambda i, be, nu: (layer, be[i], 0, 0)),
                  pl.BlockSpec((bm, 1), lambda i, be, nu: (i, 0))],
        out_specs=pl.BlockSpec((bm, d), lambda i, be, nu: (i, 0)),
        scratch_shapes=[pltpu.VMEM((d, ff), BF16), pltpu.VMEM((d, ff), BF16),
                        pltpu.VMEM((ff, d), BF16)],
    )
    cost = pl.CostEstimate(flops=6 * rows * d * ff, transcendentals=rows * ff,
                           bytes_accessed=4 * rows * d + 12 * N_EXPERTS * d * ff)
    return pl.pallas_call(
        _expert_kernel,
        out_shape=jax.ShapeDtypeStruct((rows, d), BF16),
        grid_spec=grid_spec,
        compiler_params=_cparams("arbitrary"),
        cost_estimate=cost,
        name="routed_experts",
    )(block_expert, n_used, xs, w_gate, w_up, w_down, buf_w.reshape(rows, 1))


def _take_rows(arr, idx):
    return arr.at[idx].get(mode="promise_in_bounds")


def _dispatch_plan(idx_t, gate_t, bm):
    k, t = idx_t.shape
    a = k * t
    experts = jnp.arange(N_EXPERTS, dtype=jnp.int32)
    flat_e = idx_t.reshape(a)
    gates = gate_t.reshape(a)
    iota = jnp.arange(a, dtype=jnp.int32)
    _, order = lax.sort_key_val(flat_e, iota)
    _, inv = lax.sort_key_val(order, iota)
    counts = jnp.sum((flat_e[:, None] == experts[None, :]).astype(jnp.int32), axis=0)
    starts = jnp.cumsum(counts) - counts
    padded = (counts + bm - 1) // bm * bm
    padded_end = jnp.cumsum(padded)
    padded_start = padded_end - padded
    nb = -(-a // bm) + N_EXPERTS
    n_used = (padded_end[-1:] // bm).astype(jnp.int32)
    blk = jnp.arange(nb, dtype=jnp.int32)
    block_expert = jnp.minimum(
        jnp.sum((blk[:, None] * bm >= padded_end[None, :]).astype(jnp.int32), axis=1), N_EXPERTS - 1)
    onehot = (block_expert[:, None] == experts[None, :]).astype(jnp.int32)
    b_pstart, b_count, b_start = [jnp.sum(onehot * v[None, :], axis=1)
                                  for v in (padded_start, counts, starts)]
    rank = blk[:, None] * bm + jnp.arange(bm, dtype=jnp.int32)[None, :] - b_pstart[:, None]
    valid = (rank < b_count[:, None]).reshape(nb * bm)
    src = jnp.clip(b_start[:, None] + rank, 0, a - 1).reshape(nb * bm)
    asg = _take_rows(order, src)
    buf_tok = jnp.where(valid, asg % t, 0)
    buf_w = jnp.where(valid, _take_rows(gates, asg), 0.0)
    shift = padded_start - starts
    pos = inv + _take_rows(shift, flat_e)
    return buf_tok, buf_w, pos, block_expert.astype(jnp.int32), n_used


def _swiglu_up_kernel(a_ref, wg_ref, wu_ref, o_ref):
    a = a_ref[...]
    g = jnp.dot(a, wg_ref[...], preferred_element_type=F32)
    u = jnp.dot(a, wu_ref[...], preferred_element_type=F32)
    o_ref[...] = (_silu(g) * u).astype(o_ref.dtype)


def _swiglu_up(a, wg, wu, layer):
    m, k = a.shape
    n = wg.shape[2]
    tm = _div_tile(m, MM_TM)
    tn = _div_tile(n, 512)
    wsp = pl.BlockSpec((None, k, tn), lambda i, j: (layer, 0, j))
    return pl.pallas_call(
        _swiglu_up_kernel,
        out_shape=jax.ShapeDtypeStruct((m, n), BF16),
        grid=(m // tm, n // tn),
        in_specs=[pl.BlockSpec((tm, k), lambda i, j: (i, 0)), wsp, wsp],
        out_specs=pl.BlockSpec((tm, tn), lambda i, j: (i, j)),
        compiler_params=_cparams("parallel", "arbitrary"),
        name="shared_swiglu_up",
    )(a, wg, wu)


def _lambda_init_for(layer):
    return 0.8 - 0.6 * math.exp(-0.3 * layer)


def kernel(x, c, ctx, c_ctx, w_ada, b_ada, norm_mix, norm_ffn, w_in, hgrn_lb, hgrn_norm, diff_lambda, diff_subln, w_branch, w_mgate, b_mgate, w_out, w_router, router_bias, w_exp_gate, w_exp_up, w_exp_down, w_sh_gate, w_sh_up, w_sh_down, norm_final):
    batch, seq, d = x.shape
    ctx_len = ctx.shape[1]
    depth = w_ada.shape[0]
    n_lat, n_ctx = batch * seq, batch * ctx_len
    n_all = n_lat + n_ctx
    assert batch < 8 and seq % GRID_W == 0 and ctx_len % 8 == 0

    p_lb = jax.nn.softmax(hgrn_lb.astype(F32), axis=1)
    lower_bounds = jnp.cumsum(p_lb, axis=1) - p_lb[:, :1]

    c8 = jnp.zeros((8, d), F32).at[:batch].set(c).at[batch].set(c_ctx)
    mod = _ada_mod(c8, w_ada, b_ada)

    rope_tm = _div_tile(math.gcd(seq, n_ctx), ROW_TILE)
    rope_tabs = _rope_tables(seq, rope_tm)
    hg_tabs = _hgrn_tables(_div_tile(math.gcd(seq, ctx_len), HG_CHUNK))
    rt_tabs = _ret_tables(_div_tile(math.gcd(seq, ctx_len), RT_CHUNK))

    w_in_b, w_mgate_b, w_branch_b, w_out_b = [w.astype(BF16) for w in (w_in, w_mgate, w_branch, w_out)]
    w_sg_b, w_su_b, w_sd_b = [w.astype(BF16) for w in (w_sh_gate, w_sh_up, w_sh_down)]

    xa = jnp.concatenate([x.reshape(n_lat, d), ctx.reshape(n_ctx, d)], axis=0)
    for l in range(depth):
        need_ctx = l < depth - 1
        rows = n_all if need_ctx else n_lat
        sh1, sc1, g1, sh2, sc2, g2 = [mod[l, :, i * d:(i + 1) * d].reshape(8, 1, d) for i in range(6)]

        h = _norm_mod(xa, norm_mix[l], sc1, sh1, n_all, seq, n_lat, batch, BF16)
        p = _matmul(h, w_in_b, l, BF16)
        qd, kd, qr, kr = _rope_prep(p, rope_tabs, rope_tm, seq, n_lat)
        y_hg, yc_hg = _hgrn(p, lower_bounds[0, l], lower_bounds[1, l], hgrn_norm[l], hg_tabs,
                            batch, seq, ctx_len, need_ctx)
        lam_init = _lambda_init_for(l)
        y_da = _diff_attn(qd, kd, p, diff_lambda[l], diff_subln[l], lam_init, batch, seq, ctx_len, True)
        y_rt, yc_rt = _retention(qr, kr, p, rt_tabs, batch, seq, ctx_len, need_ctx)
        ys = [y_hg, y_da, y_rt]
        if need_ctx:
            yc_da = _diff_attn(qd, kd, p, diff_lambda[l], diff_subln[l], lam_init, batch, seq,
                               ctx_len, False)
            ys = [jnp.concatenate([a, b], axis=0) for a, b in zip(ys, (yc_hg, yc_da, yc_rt))]
        merged = _merge(h, ys, w_mgate_b, b_mgate, w_branch_b, l, rows)
        tm = _proj_tile(rows, seq)
        n_tiles = rows // tm
        xa = _proj_residual(merged, w_out_b, l, xa, g1, None, 0, n_tiles, tm, seq, n_lat, batch)

        h2, idx_t, gate_t = _norm_router(xa, norm_ffn[l], sc2, sh2, w_router[l], router_bias[l],
                                         rows, seq, n_lat, batch)
        up = _swiglu_up(h2, w_sg_b, w_su_b, l)
        bounds = [n_tiles * g // MOE_GROUPS for g in range(MOE_GROUPS + 1)]
        groups = [(t0, t1 - t0) for t0, t1 in zip(bounds[:-1], bounds[1:]) if t1 > t0]
        plans, gathered, y_slots = [], [], []
        for t0, nt in groups:
            cols = slice(t0 * tm, (t0 + nt) * tm)
            plans.append(_dispatch_plan(idx_t[:, cols], gate_t[:, cols], MOE_BM))
        for (t0, nt), plan in zip(groups, plans):
            gathered.append(_take_rows(h2, plan[0] + t0 * tm))
        for (t0, nt), plan, xs in zip(groups, plans, gathered):
            buf_tok, buf_w, pos, block_expert, n_used = plan
            y_sorted = _experts(xs, w_exp_gate, w_exp_up, w_exp_down, l, buf_w, block_expert,
                                n_used, MOE_BM)
            y_slots.append(_take_rows(y_sorted, pos).reshape(TOP_K, nt * tm, d))
        parts = [_proj_residual(up, w_sd_b, l, xa, g2, ys_g, t0, nt, tm, seq, n_lat, batch)
                 for (t0, nt), ys_g in zip(groups, y_slots)]
        xa = parts[0] if len(parts) == 1 else jnp.concatenate(parts, axis=0)

    zeros = jnp.zeros((8, 1, d), F32)
    out = _norm_mod(xa, norm_final, zeros, zeros, n_lat, seq, n_lat, batch, F32)
    return out.reshape(batch, seq, d)
```

```python
import functools
import math

import numpy as np
import jax
import jax.numpy as jnp
from jax import lax
from jax.experimental import pallas as pl
from jax.experimental.pallas import tpu as pltpu

F32 = jnp.float32
BF16 = jnp.bfloat16
FP8 = jnp.float8_e4m3fn
FP8_MAX = 448.0

NORM_EPS = 1e-6
ROPE_BASE = 10000.0
GRID_W = 64
HG_HEADS = 8
DA_HEADS = 8
DA_DH = 64
RT_HEADS = 8
RT_DK = 64
N_EXPERTS = 64
TOP_K = 8
N_GROUPS = 8
TOPK_GROUPS = 4
EXPERT_FF = 256
ROUTED_SCALE = 2.5
BRANCH_W = 1024

LANES = 128
V7X_VMEM_BYTES = 64 * 1024 * 1024
VMEM_LIMIT = 56 * 1024 * 1024

COL_HG_Q, COL_HG_FF, COL_HG_FB, COL_HG_I, COL_HG_G = 0, 8, 16, 24, 32
COL_DA_Q, COL_DA_K, COL_DA_V = 40, 48, 56
COL_RT_Q, COL_RT_K, COL_RT_V, COL_RT_G = 64, 68, 72, 80

NT_DIMS = (((1,), (1,)), ((), ()))
TN_DIMS = (((0,), (0,)), ((), ()))

ROW_TILE = 256
MM_TM = 1024
HG_CHUNK = 128
RT_CHUNK = 256
ATT_TQ = 512
ATT_TK = 512
ATT_VROWS = LANES + 16
MOE_BM = 256
MOE_GROUPS = 1


def _cparams(*sem):
    return pltpu.CompilerParams(dimension_semantics=sem, vmem_limit_bytes=VMEM_LIMIT)


def _div_tile(n, pref):
    t = min(n, pref)
    while n % t:
        t //= 2
    return t


def _sigmoid(x):
    return 1.0 / (1.0 + jnp.exp(-x))


def _silu(x):
    return x * _sigmoid(x)


def _ada_kernel(c_ref, w_ref, b_ref, o_ref):
    a = _silu(c_ref[...]).astype(BF16)
    o_ref[...] = jnp.dot(a, w_ref[...].astype(BF16), preferred_element_type=F32) + b_ref[...]


def _ada_mod(c8, w_ada, b_ada):
    depth, d, n = w_ada.shape
    tn = _div_tile(n, 512)
    return pl.pallas_call(
        _ada_kernel,
        out_shape=jax.ShapeDtypeStruct((depth, 8, n), F32),
        grid=(depth, n // tn),
        in_specs=[
            pl.BlockSpec((8, d), lambda l, j: (0, 0)),
            pl.BlockSpec((None, d, tn), lambda l, j: (l, 0, j)),
            pl.BlockSpec((None, 1, tn), lambda l, j: (l, 0, j)),
        ],
        out_specs=pl.BlockSpec((None, 8, tn), lambda l, j: (l, 0, j)),
        compiler_params=_cparams("arbitrary", "arbitrary"),
        name="ada_mod",
    )(c8, w_ada, b_ada.reshape(depth, 1, n))


def _mod_row_map(tm, seq, n_lat, batch):
    def index_map(i):
        start = i * tm
        return (jnp.where(start < n_lat, start // seq, batch), 0, 0)
    return index_map


def _norm_mod_kernel(quantize, x_ref, w_ref, sc_ref, sh_ref, o_ref, *q_refs):
    x = x_ref[...]
    ms = jnp.mean(x * x, axis=-1, keepdims=True)
    y = x * lax.rsqrt(ms + NORM_EPS) * w_ref[...]
    h = y * (1.0 + sc_ref[...]) + sh_ref[...]
    o_ref[...] = h.astype(o_ref.dtype)
    if quantize:
        q_ref, s_ref = q_refs
        amax = jnp.max(jnp.abs(h), axis=-1, keepdims=True)
        scale = jnp.where(amax > 0.0, amax * (1.0 / FP8_MAX), 1.0)
        q_ref[...] = (h / scale).astype(FP8)
        s_ref[...] = scale


def _norm_mod(x, w, sc, sh, rows, seq, n_lat, batch, out_dtype, quantize=False):
    d = x.shape[1]
    tm = _div_tile(math.gcd(seq, rows), ROW_TILE)
    rmap = _mod_row_map(tm, seq, n_lat, batch)
    row_blk = pl.BlockSpec((tm, d), lambda i: (i, 0))
    out_shape = [jax.ShapeDtypeStruct((rows, d), out_dtype)]
    out_specs = [row_blk]
    if quantize:
        out_shape += [jax.ShapeDtypeStruct((rows, d), FP8), jax.ShapeDtypeStruct((rows, 1), F32)]
        out_specs += [row_blk, pl.BlockSpec((tm, 1), lambda i: (i, 0))]
    outs = pl.pallas_call(
        functools.partial(_norm_mod_kernel, quantize),
        out_shape=tuple(out_shape),
        grid=(rows // tm,),
        in_specs=[
            row_blk,
            pl.BlockSpec((1, d), lambda i: (0, 0)),
            pl.BlockSpec((None, 1, d), rmap),
            pl.BlockSpec((None, 1, d), rmap),
        ],
        out_specs=tuple(out_specs),
        compiler_params=_cparams("parallel"),
        name="norm_mod",
    )(x, w.reshape(1, d), sc, sh)
    return outs if quantize else outs[0]


def _mm_kernel(a_ref, b_ref, o_ref):
    o_ref[...] = jnp.dot(a_ref[...], b_ref[...], preferred_element_type=F32).astype(o_ref.dtype)


def _matmul(a, b, layer, out_dtype, tn_pref=1024):
    m, k = a.shape
    n = b.shape[2]
    tm = _div_tile(m, MM_TM)
    tn = _div_tile(n, tn_pref)
    return pl.pallas_call(
        _mm_kernel,
        out_shape=jax.ShapeDtypeStruct((m, n), out_dtype),
        grid=(m // tm, n // tn),
        in_specs=[pl.BlockSpec((tm, k), lambda i, j: (i, 0)),
                  pl.BlockSpec((None, k, tn), lambda i, j: (layer, 0, j))],
        out_specs=pl.BlockSpec((tm, tn), lambda i, j: (i, j)),
        compiler_params=_cparams("parallel", "arbitrary"),
        name="matmul",
    )(a, b)


def _rope_tables(seq, n_ctx):
    pos = np.arange(seq, dtype=np.float64)
    rows = np.floor(pos / GRID_W)
    cols = pos - rows * GRID_W
    lane = np.arange(LANES)
    j = lane % 64
    part, jj = j // 32, j % 32
    inv = ROPE_BASE ** (-np.arange(0, 32, 2, dtype=np.float64) / 32.0)
    ang = np.where(part[None, :] == 0, rows[:, None], cols[:, None]) * inv[jj % 16][None, :]
    cos_da = np.cos(ang)
    sin_da = np.sin(ang) * np.where(jj < 16, -1.0, 1.0)[None, :]
    partner_da = np.where(jj < 16, lane + 16, lane - 16)
    theta = 1.0 / (ROPE_BASE ** np.linspace(0.0, 1.0, 32))
    ang = pos[:, None] * theta[j % 32][None, :]
    cos_rt = np.cos(ang)
    sin_rt = np.sin(ang) * np.where(j < 32, -1.0, 1.0)[None, :]
    partner_rt = np.where(j < 32, lane + 32, lane - 32)

    def full(tab, fill):
        return jnp.asarray(np.concatenate([tab, np.full((n_ctx, LANES), fill)], 0), F32)

    def perm(partner):
        p = np.zeros((LANES, LANES), np.float32)
        p[partner, lane] = 1.0
        return jnp.asarray(p, BF16)

    return (full(cos_da, 1.0), full(sin_da, 0.0), full(cos_rt, 1.0), full(sin_rt, 0.0),
            perm(partner_da), perm(partner_rt))


def _rope_kernel(qd_ref, kd_ref, qr_ref, kr_ref, cd_ref, sd_ref, cr_ref, sr_ref, pd_ref, pr_ref,
                 oqd_ref, okd_ref, oqr_ref, okr_ref):
    def rot(src, dst, cos_ref, sin_ref, perm_ref, scale):
        cos = cos_ref[...]
        sin = sin_ref[...]
        perm = perm_ref[...]
        for g in range(src.shape[1] // LANES):
            sl = slice(g * LANES, (g + 1) * LANES)
            x = src[:, sl]
            xs = jnp.dot(x, perm, preferred_element_type=F32)
            y = x.astype(F32) * cos + xs * sin
            dst[:, sl] = (y * scale).astype(dst.dtype)

    rot(qd_ref, oqd_ref, cd_ref, sd_ref, pd_ref, DA_DH ** -0.5 * math.log2(math.e))
    rot(kd_ref, okd_ref, cd_ref, sd_ref, pd_ref, 1.0)
    rot(qr_ref, oqr_ref, cr_ref, sr_ref, pr_ref, 1.0)
    rot(kr_ref, okr_ref, cr_ref, sr_ref, pr_ref, RT_DK ** -0.5)


def _rope_prep(p, tables, tm, seq, n_lat):
    rows = p.shape[0]
    cos_da, sin_da, cos_rt, sin_rt, perm_da, perm_rt = tables
    lat_tiles, seq_tiles = n_lat // tm, seq // tm

    def tmap(i):
        return (jnp.where(i < lat_tiles, i % seq_tiles, seq_tiles), 0)

    wd, wr = DA_HEADS * 2 * DA_DH, RT_HEADS * RT_DK
    tab = pl.BlockSpec((tm, LANES), tmap)
    pm = pl.BlockSpec((LANES, LANES), lambda i: (0, 0))
    return pl.pallas_call(
        _rope_kernel,
        out_shape=(jax.ShapeDtypeStruct((rows, wd), BF16), jax.ShapeDtypeStruct((rows, wd), BF16),
                   jax.ShapeDtypeStruct((rows, wr), BF16), jax.ShapeDtypeStruct((rows, wr), BF16)),
        grid=(rows // tm,),
        in_specs=[
            pl.BlockSpec((tm, wd), lambda i: (i, COL_DA_Q * LANES // wd)),
            pl.BlockSpec((tm, wd), lambda i: (i, COL_DA_K * LANES // wd)),
            pl.BlockSpec((tm, wr), lambda i: (i, COL_RT_Q * LANES // wr)),
            pl.BlockSpec((tm, wr), lambda i: (i, COL_RT_K * LANES // wr)),
            tab, tab, tab, tab, pm, pm,
        ],
        out_specs=(pl.BlockSpec((tm, wd), lambda i: (i, 0)), pl.BlockSpec((tm, wd), lambda i: (i, 0)),
                   pl.BlockSpec((tm, wr), lambda i: (i, 0)), pl.BlockSpec((tm, wr), lambda i: (i, 0))),
        compiler_params=_cparams("parallel"),
        name="rope_prep",
    )(p, p, p, p, cos_da, sin_da, cos_rt, sin_rt, perm_da, perm_rt)


def _hgrn_tables(c):
    nlev = int(math.log2(c))
    m = np.zeros((nlev + 2, c, c), np.float32)
    w = np.zeros((nlev + 1, c, c), np.float32)
    for l in range(nlev):
        bs = c >> (l + 1)
        for t in range(c):
            blk = t // bs
            if blk % 2 == 1:
                m[l, t, blk * bs:t + 1] = 1.0
                w[l, t, (blk - 1) * bs:blk * bs] = 1.0
            else:
                m[l, t, t + 1:(blk + 1) * bs] = 1.0
    for t in range(c):
        m[nlev, t, :t + 1] = 1.0
        m[nlev + 1, t, t + 1:] = 1.0
    w[nlev] = np.eye(c)
    mb = m[:, ::-1, ::-1]
    wb = w[:, ::-1, ::-1]
    r = (nlev + 2) * c
    return (jnp.asarray(m.reshape(r, c), BF16), jnp.asarray(mb.reshape(r, c), BF16),
            jnp.asarray(w, F32), jnp.asarray(wb, F32))


def _hgrn_chunk(q_raw, f_raw, v, lb, m_ref, w_ref, st, fwd):
    c = q_raw.shape[0]
    nlev = w_ref.shape[0] - 1
    q = _silu(q_raw.astype(F32))
    f = lb + (1.0 - lb) * _sigmoid(f_raw.astype(F32))
    lf = jnp.log(f)
    k = 1.0 - f
    hi = lf.astype(BF16)
    lo = (lf - hi.astype(F32)).astype(BF16)
    g2 = jnp.dot(m_ref[...], jnp.concatenate([hi, lo], axis=1), preferred_element_type=F32)
    g = g2[:, :LANES] + g2[:, LANES:]
    e = jnp.exp(g)
    row = lax.broadcasted_iota(jnp.int32, (c, LANES), 0)
    a = w_ref[nlev] * lax.dot_general(q.astype(BF16), k.astype(BF16), NT_DIMS,
                                      preferred_element_type=F32)
    for l in range(nlev):
        shift = int(math.log2(c >> (l + 1)))
        odd = ((row >> shift) & 1) == 1
        x = (jnp.where(odd, q, k) if fwd else jnp.where(odd, k, q)) * e[l * c:(l + 1) * c]
        xb = x.astype(BF16)
        a = a + w_ref[l] * lax.dot_general(xb, xb, NT_DIMS, preferred_element_type=F32)
    qe = (q * e[nlev * c:(nlev + 1) * c]).astype(BF16)
    ke = (k * e[(nlev + 1) * c:(nlev + 2) * c]).astype(BF16)
    o = (jnp.dot(a.astype(BF16), v, preferred_element_type=F32)
         + lax.dot_general(qe, st.astype(BF16), NT_DIMS, preferred_element_type=F32))
    last = nlev * c + (c - 1 if fwd else 0)
    dec = jnp.exp(g[last:last + 1, :])
    st_new = dec * st + lax.dot_general(v, ke, TN_DIMS, preferred_element_type=F32)
    return o, st_new


def _hgrn_kernel(need_ctx, chunk, *refs):
    (ql, ffl, fbl, il, gl, qc, ffc, fbc, ic, gc, lbf_ref, lbb_ref, nw_ref,
     mf_ref, mb_ref, wf_ref, wb_ref) = refs[:17]
    rest = refs[17:]
    if need_ctx:
        yl_ref, yc_ref, of_ref, ob_ref, ocf_ref, ocb_ref, st_ref = rest
    else:
        yl_ref, of_ref, ob_ref, st_ref = rest
        yc_ref = ocf_ref = ocb_ref = None
    nc_lat = ql.shape[0] // chunk
    nc_ctx = qc.shape[0] // chunk
    lbf = lbf_ref[...]
    lbb = lbb_ref[...]
    st_ref[...] = jnp.zeros_like(st_ref)

    def step(j, q_ref, ff_ref, fb_ref, i_ref, n, outf, outb):
        sf = pl.ds(pl.multiple_of(j * chunk, chunk), chunk)
        sb = pl.ds(pl.multiple_of((n - 1 - j) * chunk, chunk), chunk)
        o, s = _hgrn_chunk(q_ref[sf, :], ff_ref[sf, :], i_ref[sf, :], lbf, mf_ref, wf_ref,
                           st_ref[0], True)
        st_ref[0] = s
        if outf is not None:
            outf[sf, :] = o
        o, s = _hgrn_chunk(q_ref[sb, :], fb_ref[sb, :], i_ref[sb, :], lbb, mb_ref, wb_ref,
                           st_ref[1], False)
        st_ref[1] = s
        if outb is not None:
            outb[sb, :] = o

    def ctx_body(j, carry):
        step(j, qc, ffc, fbc, ic, nc_ctx, ocf_ref, ocb_ref)
        return carry

    def lat_body(j, carry):
        step(j, ql, ffl, fbl, il, nc_lat, of_ref, ob_ref)
        return carry

    lax.fori_loop(0, nc_ctx, ctx_body, 0)
    lax.fori_loop(0, nc_lat, lat_body, 0, unroll=2)

    nw = nw_ref[...]

    def readout(n, a_ref, b_ref, g_ref, y_ref):
        def body(j, carry):
            s = pl.ds(pl.multiple_of(j * chunk, chunk), chunk)
            o = a_ref[s, :] + b_ref[s, :]
            o = o * lax.rsqrt(jnp.mean(o * o, axis=-1, keepdims=True) + NORM_EPS) * nw
            y_ref[s, :] = (o * _silu(g_ref[s, :].astype(F32))).astype(y_ref.dtype)
            return carry
        lax.fori_loop(0, n, body, 0)

    readout(nc_lat, of_ref, ob_ref, gl, yl_ref)
    if need_ctx:
        readout(nc_ctx, ocf_ref, ocb_ref, gc, yc_ref)


def _hgrn(p, lb_f, lb_b, norm_w, tables, batch, seq, ctx_len, need_ctx):
    n_lat = batch * seq
    chunk = _div_tile(math.gcd(seq, ctx_len), HG_CHUNK)
    mf, mb, wf, wb = tables
    ctx_blk0 = n_lat // ctx_len
    width = HG_HEADS * LANES

    def lat(col):
        return pl.BlockSpec((seq, LANES), lambda b, h: (b, col + h))

    def ctx(col):
        return pl.BlockSpec((ctx_len, LANES), lambda b, h: (ctx_blk0 + b, col + h))

    head_vec = pl.BlockSpec((1, LANES), lambda b, h: (0, h))
    const2 = lambda arr: pl.BlockSpec(arr.shape, lambda b, h: (0,) * arr.ndim)
    cols = (COL_HG_Q, COL_HG_FF, COL_HG_FB, COL_HG_I, COL_HG_G)
    in_specs = ([lat(c) for c in cols] + [ctx(c) for c in cols]
                + [head_vec, head_vec, pl.BlockSpec((1, LANES), lambda b, h: (0, 0)),
                   const2(mf), const2(mb), const2(wf), const2(wb)])
    out_shape = [jax.ShapeDtypeStruct((n_lat, width), BF16)]
    out_specs = [pl.BlockSpec((seq, LANES), lambda b, h: (b, h))]
    scratch = [pltpu.VMEM((seq, LANES), F32), pltpu.VMEM((seq, LANES), F32)]
    if need_ctx:
        out_shape.append(jax.ShapeDtypeStruct((batch * ctx_len, width), BF16))
        out_specs.append(pl.BlockSpec((ctx_len, LANES), lambda b, h: (b, h)))
        scratch += [pltpu.VMEM((ctx_len, LANES), F32), pltpu.VMEM((ctx_len, LANES), F32)]
    scratch.append(pltpu.VMEM((2, LANES, LANES), F32))
    outs = pl.pallas_call(
        functools.partial(_hgrn_kernel, need_ctx, chunk),
        out_shape=tuple(out_shape),
        grid=(batch, HG_HEADS),
        in_specs=in_specs,
        out_specs=tuple(out_specs),
        scratch_shapes=scratch,
        compiler_params=_cparams("parallel", "parallel"),
        name="hgrn2_scan",
    )(*([p] * 10), lb_f.reshape(1, width), lb_b.reshape(1, width), norm_w.reshape(1, LANES),
      mf, mb, wf, wb)
    return outs[0], (outs[1] if need_ctx else None)


def _attn_kernel(has_lat, tk, lambda_init, *refs):
    if has_lat:
        (q_ref, kc_ref, vc_ref, kl_ref, vl_ref, lam_ref, sub_ref, o_ref,
         s_ref, vtc_ref, vtl_ref) = refs
    else:
        q_ref, kc_ref, vc_ref, lam_ref, sub_ref, o_ref, s_ref, vtc_ref = refs
    tq = q_ref.shape[0]
    w2 = 2 * tq
    nctx = kc_ref.shape[0]
    nlat = kl_ref.shape[0] if has_lat else 0
    n_chunks = nlat // tk
    sub_t = min(256, tk)

    @pl.when(pl.program_id(2) == 0)
    def _():
        vtc_ref[LANES:, :] = jnp.ones((ATT_VROWS - LANES, nctx), BF16)
        for r in range(0, nctx, min(sub_t, nctx)):
            rr = slice(r, r + min(sub_t, nctx))
            vtc_ref[:LANES, rr] = vc_ref[rr, :].astype(F32).T.astype(BF16)
        for j in range(n_chunks):
            vtl_ref[j, LANES:, :] = jnp.ones((ATT_VROWS - LANES, tk), BF16)
            for r in range(0, tk, sub_t):
                vtl_ref[j, :LANES, r:r + sub_t] = (
                    vl_ref[j * tk + r:j * tk + r + sub_t, :].astype(F32).T.astype(BF16))

    q = q_ref[...].astype(F32).T
    sub = lax.broadcasted_iota(jnp.int32, (LANES, tq), 0)
    qst = jnp.concatenate([jnp.where(sub < DA_DH, q, 0.0), jnp.where(sub >= DA_DH, q, 0.0)],
                          axis=1).astype(BF16)

    def col_fold(x, op):
        return op(x.reshape(x.shape[0] // 8, 8, w2), axis=0)

    def scores(k_ref, src, dst, macc):
        s = jnp.dot(k_ref[src, :], qst, preferred_element_type=F32)
        s_ref[dst, :] = s
        return jnp.maximum(macc, col_fold(s, jnp.max))

    macc = scores(kc_ref, slice(0, nctx), slice(0, nctx), jnp.full((8, w2), -jnp.inf, F32))
    for j in range(n_chunks):
        macc = scores(kl_ref, slice(j * tk, (j + 1) * tk), slice(nctx + j * tk, nctx + (j + 1) * tk),
                      macc)
    m = jnp.max(macc, axis=0, keepdims=True)

    def weights(rows):
        return jnp.exp2((s_ref[rows, :] - m).astype(BF16))

    pv = jnp.dot(vtc_ref[...], weights(slice(0, nctx)), preferred_element_type=F32)
    for j in range(n_chunks):
        pv = pv + jnp.dot(vtl_ref[j], weights(slice(nctx + j * tk, nctx + (j + 1) * tk)),
                          preferred_element_type=F32)
    on = pv[:LANES] / pv[LANES:LANES + 1]
    lv = lam_ref[...]
    lam = (jnp.exp(jnp.sum(lv[0:1] * lv[1:2], axis=-1, keepdims=True))
           - jnp.exp(jnp.sum(lv[2:3] * lv[3:4], axis=-1, keepdims=True)) + lambda_init)
    o = on[:, :tq] - lam * on[:, tq:]
    o = o * lax.rsqrt(jnp.mean(o * o, axis=0, keepdims=True) + NORM_EPS) * sub_ref[...]
    o_ref[...] = (o * (1.0 - lambda_init)).T.astype(o_ref.dtype)


def _diff_attn(qd, kd, p, lam_vec, subln_w, lambda_init, batch, seq, ctx_len, latent_queries):
    n_lat = batch * seq
    ctx_blk0 = n_lat // ctx_len
    width = DA_HEADS * LANES
    lq = seq if latent_queries else ctx_len
    tq = _div_tile(lq, ATT_TQ)
    q_blk0 = 0 if latent_queries else n_lat // tq
    qt = lq // tq
    tk = _div_tile(seq, ATT_TK)
    in_specs = [
        pl.BlockSpec((tq, LANES), lambda b, h, i: (q_blk0 + b * qt + i, h)),
        pl.BlockSpec((ctx_len, LANES), lambda b, h, i: (ctx_blk0 + b, h)),
        pl.BlockSpec((ctx_len, LANES), lambda b, h, i: (ctx_blk0 + b, COL_DA_V + h)),
    ]
    args = [qd, kd, p]
    n_keys = ctx_len + (seq if latent_queries else 0)
    scratch = [pltpu.VMEM((n_keys, 2 * tq), F32), pltpu.VMEM((ATT_VROWS, ctx_len), BF16)]
    if latent_queries:
        in_specs += [pl.BlockSpec((seq, LANES), lambda b, h, i: (b, h)),
                     pl.BlockSpec((seq, LANES), lambda b, h, i: (b, COL_DA_V + h))]
        args += [kd, p]
        scratch.append(pltpu.VMEM((seq // tk, ATT_VROWS, tk), BF16))
    in_specs += [pl.BlockSpec((4, DA_DH), lambda b, h, i: (0, 0)),
                 pl.BlockSpec((LANES, 1), lambda b, h, i: (0, 0))]
    args += [lam_vec, subln_w.reshape(LANES, 1)]
    return pl.pallas_call(
        functools.partial(_attn_kernel, latent_queries, tk, lambda_init),
        out_shape=jax.ShapeDtypeStruct((batch * lq, width), BF16),
        grid=(batch, DA_HEADS, qt),
        in_specs=in_specs,
        out_specs=pl.BlockSpec((tq, LANES), lambda b, h, i: (b * qt + i, h)),
        scratch_shapes=scratch,
        compiler_params=_cparams("parallel", "parallel", "arbitrary"),
        name="diff_attn_lat" if latent_queries else "diff_attn_ctx",
    )(*args)


def _ret_tables(c):
    gam = 1.0 - 2.0 ** (-5.0 - np.arange(RT_HEADS, dtype=np.float64))
    idx = np.arange(c, dtype=np.float64)
    dist = np.abs(idx[:, None] - idx[None, :])
    dsym = gam[:, None, None] ** dist[None] * np.where(dist == 0, 2.0, 1.0)[None]
    lane_head = np.arange(LANES) // RT_DK
    mask = (lane_head[None, :] == (np.arange(RT_HEADS) % 2)[:, None]).astype(np.float64)

    def tab(power):
        return (gam[:, None] ** power[None, :])[:, :, None] * mask[:, None, :]

    qdf, kdf = tab(idx + 1.0), tab(c - 1.0 - idx)
    qdb, kdb = tab(c - idx), tab(idx)
    cdec = np.broadcast_to((gam ** c)[:, None, None], (RT_HEADS, 1, LANES))
    hp = RT_HEADS // 2
    f = lambda x: jnp.asarray(np.ascontiguousarray(x).reshape((hp, 2) + x.shape[1:]), F32)
    return f(dsym), f(mask[:, None, :]), f(qdf), f(kdf), f(qdb), f(kdb), f(cdec)


def _ret_kernel(need_ctx, chunk, *refs):
    (ql, kl, vl, gl, qc, kc, vc, gc, dsym_ref, msk_ref, qdf_ref, kdf_ref, qdb_ref, kdb_ref,
     cdec_ref) = refs[:15]
    rest = refs[15:]
    if need_ctx:
        yl_ref, yc_ref, of_ref, ob_ref, ocf_ref, ocb_ref, st_ref = rest
    else:
        yl_ref, of_ref, ob_ref, st_ref = rest
        yc_ref = ocf_ref = ocb_ref = None
    nc_lat = ql.shape[0] // chunk
    nc_ctx = qc.shape[0] // chunk
    st_ref[...] = jnp.zeros_like(st_ref)

    def step(j, q_ref, k_ref, v_ref, n, outf, outb):
        sf = pl.ds(pl.multiple_of(j * chunk, chunk), chunk)
        sb = pl.ds(pl.multiple_of((n - 1 - j) * chunk, chunk), chunk)
        qf = q_ref[sf, :].astype(F32)
        kfb = k_ref[sf, :]
        kf = kfb.astype(F32)
        qb = q_ref[sb, :].astype(F32)
        kb = k_ref[sb, :].astype(F32)
        for h in range(2):
            hs = slice(h * LANES, (h + 1) * LANES)
            vf = v_ref[sf, hs]
            vb = v_ref[sb, hs]
            a = lax.dot_general((qf * msk_ref[h]).astype(BF16), kfb, NT_DIMS,
                                preferred_element_type=F32) * dsym_ref[h]
            o = (jnp.dot(a.astype(BF16), vf, preferred_element_type=F32)
                 + jnp.dot((qf * qdf_ref[h]).astype(BF16), st_ref[h].astype(BF16),
                           preferred_element_type=F32))
            st_ref[h] = cdec_ref[h] * st_ref[h] + lax.dot_general(
                (kf * kdf_ref[h]).astype(BF16), vf, TN_DIMS, preferred_element_type=F32)
            if outf is not None:
                outf[sf, hs] = o
            o = jnp.dot((qb * qdb_ref[h]).astype(BF16), st_ref[2 + h].astype(BF16),
                        preferred_element_type=F32)
            st_ref[2 + h] = cdec_ref[h] * st_ref[2 + h] + lax.dot_general(
                (kb * kdb_ref[h]).astype(BF16), vb, TN_DIMS, preferred_element_type=F32)
            if outb is not None:
                outb[sb, hs] = o

    def ctx_body(j, carry):
        step(j, qc, kc, vc, nc_ctx, ocf_ref, ocb_ref)
        return carry

    def lat_body(j, carry):
        step(j, ql, kl, vl, nc_lat, of_ref, ob_ref)
        return carry

    lax.fori_loop(0, nc_ctx, ctx_body, 0)
    lax.fori_loop(0, nc_lat, lat_body, 0)

    def readout(n, a_ref, b_ref, g_ref, y_ref):
        def body(j, carry):
            s = pl.ds(pl.multiple_of(j * chunk, chunk), chunk)
            for h in range(2):
                hs = slice(h * LANES, (h + 1) * LANES)
                o = a_ref[s, hs] + b_ref[s, hs]
                o = o * lax.rsqrt(jnp.mean(o * o, axis=-1, keepdims=True) + NORM_EPS)
                y_ref[s, hs] = (o * _silu(g_ref[s, hs].astype(F32))).astype(y_ref.dtype)
            return carry
        lax.fori_loop(0, n, body, 0)

    readout(nc_lat, of_ref, ob_ref, gl, yl_ref)
    if need_ctx:
        readout(nc_ctx, ocf_ref, ocb_ref, gc, yc_ref)


def _retention(qr, kr, p, tables, batch, seq, ctx_len, need_ctx):
    n_lat = batch * seq
    chunk = _div_tile(math.gcd(seq, ctx_len), RT_CHUNK)
    ctx_blk0 = n_lat // ctx_len
    hp = RT_HEADS // 2
    pair = 2 * LANES
    width = RT_HEADS * LANES
    v_col, g_col = COL_RT_V * LANES // pair, COL_RT_G * LANES // pair

    in_specs = [
        pl.BlockSpec((seq, LANES), lambda b, h: (b, h)),
        pl.BlockSpec((seq, LANES), lambda b, h: (b, h)),
        pl.BlockSpec((seq, pair), lambda b, h: (b, v_col + h)),
        pl.BlockSpec((seq, pair), lambda b, h: (b, g_col + h)),
        pl.BlockSpec((ctx_len, LANES), lambda b, h: (ctx_blk0 + b, h)),
        pl.BlockSpec((ctx_len, LANES), lambda b, h: (ctx_blk0 + b, h)),
        pl.BlockSpec((ctx_len, pair), lambda b, h: (ctx_blk0 + b, v_col + h)),
        pl.BlockSpec((ctx_len, pair), lambda b, h: (ctx_blk0 + b, g_col + h)),
    ]
    for t in tables:
        in_specs.append(pl.BlockSpec((None,) + t.shape[1:], lambda b, h: (h, 0, 0, 0)))
    out_shape = [jax.ShapeDtypeStruct((n_lat, width), BF16)]
    out_specs = [pl.BlockSpec((seq, pair), lambda b, h: (b, h))]
    scratch = [pltpu.VMEM((seq, pair), F32), pltpu.VMEM((seq, pair), F32)]
    if need_ctx:
        out_shape.append(jax.ShapeDtypeStruct((batch * ctx_len, width), BF16))
        out_specs.append(pl.BlockSpec((ctx_len, pair), lambda b, h: (b, h)))
        scratch += [pltpu.VMEM((ctx_len, pair), F32), pltpu.VMEM((ctx_len, pair), F32)]
    scratch.append(pltpu.VMEM((4, LANES, LANES), F32))
    outs = pl.pallas_call(
        functools.partial(_ret_kernel, need_ctx, chunk),
        out_shape=tuple(out_shape),
        grid=(batch, hp),
        in_specs=in_specs,
        out_specs=tuple(out_specs),
        scratch_shapes=scratch,
        compiler_params=_cparams("parallel", "parallel"),
        name="retention_scan",
    )(qr, kr, p, p, qr, kr, p, p, *tables)
    return outs[0], (outs[1] if need_ctx else None)


def _merge_kernel(h_ref, hs_ref, y0_ref, y1_ref, y2_ref, wg_ref, ws_ref, bg_ref, wb_ref, o_ref):
    h = h_ref[...]
    hs = hs_ref[...]
    acc = None
    for i, y_ref in enumerate((y0_ref, y1_ref, y2_ref)):
        logits = jnp.dot(h, wg_ref[i], preferred_element_type=F32) * hs * ws_ref[i] + bg_ref[i]
        t = _sigmoid(logits) * jnp.dot(y_ref[...], wb_ref[i], preferred_element_type=F32)
        acc = t if acc is None else acc + t
    o_ref[...] = acc.astype(o_ref.dtype)


def _quantize_cols(w):
    amax = jnp.max(jnp.abs(w), axis=-2, keepdims=True)
    scale = jnp.where(amax > 0.0, amax * (1.0 / FP8_MAX), 1.0)
    return (w / scale).astype(FP8), scale


def _merge(h8, h_scale, ys, w_mgate8, w_mgate_scale, b_mgate, w_branch, layer, rows):
    d = h8.shape[1]
    tm = _div_tile(rows, MM_TM)
    tn = _div_tile(d, 512)
    ysp = pl.BlockSpec((tm, BRANCH_W), lambda i, j: (i, 0))
    vec = pl.BlockSpec((None, 3, 1, tn), lambda i, j: (layer, 0, 0, j))
    return pl.pallas_call(
        _merge_kernel,
        out_shape=jax.ShapeDtypeStruct((rows, d), BF16),
        grid=(rows // tm, d // tn),
        in_specs=[pl.BlockSpec((tm, d), lambda i, j: (i, 0)),
                  pl.BlockSpec((tm, 1), lambda i, j: (i, 0)), ysp, ysp, ysp,
                  pl.BlockSpec((None, 3, d, tn), lambda i, j: (layer, 0, 0, j)), vec, vec,
                  pl.BlockSpec((None, 3, BRANCH_W, tn), lambda i, j: (layer, 0, 0, j))],
        out_specs=pl.BlockSpec((tm, tn), lambda i, j: (i, j)),
        compiler_params=_cparams("parallel", "arbitrary"),
        name="branch_merge",
    )(h8, h_scale, *ys, w_mgate8, w_mgate_scale, b_mgate.reshape(b_mgate.shape[0], 3, 1, d),
      w_branch)


def _proj_res_kernel(has_extra, *refs):
    if has_extra:
        a_ref, w_ref, x_ref, g_ref, e_ref, o_ref = refs
    else:
        a_ref, w_ref, x_ref, g_ref, o_ref = refs
    y = jnp.dot(a_ref[...], w_ref[...], preferred_element_type=F32)
    if has_extra:
        for s in range(e_ref.shape[0]):
            y = y + e_ref[s].astype(F32)
    o_ref[...] = x_ref[...] + g_ref[...] * y


def _proj_tile(rows, seq):
    return _div_tile(math.gcd(seq, rows), MM_TM)


def _proj_residual(a, w, layer, x, gate, extra, tile0, n_tiles, tm, seq, n_lat, batch):
    k = a.shape[1]
    d = w.shape[2]
    tn = _div_tile(d, 512)
    rows = n_tiles * tm
    rmap = _mod_row_map(tm, seq, n_lat, batch)
    in_specs = [pl.BlockSpec((tm, k), lambda i, j: (tile0 + i, 0)),
                pl.BlockSpec((None, k, tn), lambda i, j: (layer, 0, j)),
                pl.BlockSpec((tm, tn), lambda i, j: (tile0 + i, j)),
                pl.BlockSpec((None, 1, tn), lambda i, j: rmap(tile0 + i)[:2] + (j,))]
    args = [a, w, x, gate]
    n_slots = 0
    if extra is not None:
        n_slots = extra.shape[0]
        in_specs.append(pl.BlockSpec((n_slots, tm, tn), lambda i, j: (0, i, j)))
        args.append(extra)
    cost = pl.CostEstimate(flops=2 * rows * k * d, transcendentals=0,
                           bytes_accessed=rows * d * (8 + 2 * n_slots) + 2 * rows * k
                           + 2 * k * d * n_tiles)
    return pl.pallas_call(
        functools.partial(_proj_res_kernel, extra is not None),
        out_shape=jax.ShapeDtypeStruct((rows, d), F32),
        grid=(n_tiles, d // tn),
        in_specs=in_specs,
        out_specs=pl.BlockSpec((tm, tn), lambda i, j: (i, j)),
        compiler_params=_cparams("parallel", "arbitrary"),
        cost_estimate=cost,
        name="proj_residual",
    )(*args)


def _router_kernel(x_ref, w_ref, sc_ref, sh_ref, wr_ref, rb_ref, h_ref, idx_ref, gate_ref):
    x = x_ref[...]
    ms = jnp.mean(x * x, axis=-1, keepdims=True)
    h = x * lax.rsqrt(ms + NORM_EPS) * w_ref[...] * (1.0 + sc_ref[...]) + sh_ref[...]
    h_ref[...] = h.astype(h_ref.dtype)
    tm = x.shape[0]
    logits = lax.dot_general(wr_ref[...], h, NT_DIMS, precision=lax.Precision.HIGHEST,
                             preferred_element_type=F32)
    scores = _sigmoid(logits)
    choice = scores + rb_ref[...]
    per_group = N_EXPERTS // N_GROUPS
    neg = -jnp.inf
    sub = lax.broadcasted_iota(jnp.int32, (per_group, tm), 0).astype(F32)
    gs = []
    for g in range(N_GROUPS):
        xg = choice[g * per_group:(g + 1) * per_group, :]
        m1 = jnp.max(xg, axis=0, keepdims=True)
        i1 = jnp.min(jnp.where(xg == m1, sub, float(per_group)), axis=0, keepdims=True)
        m2 = jnp.max(jnp.where(sub == i1, neg, xg), axis=0, keepdims=True)
        gs.append(m1 + m2)
    sel = [jnp.zeros((1, tm), F32) for _ in range(N_GROUPS)]
    for _ in range(TOPK_GROUPS):
        m = functools.reduce(jnp.maximum, gs)
        found = jnp.zeros((1, tm), F32)
        for g in range(N_GROUPS):
            hit = jnp.where(gs[g] == m, 1.0 - found, 0.0)
            sel[g] = sel[g] + hit
            found = found + hit
            gs[g] = jnp.where(hit > 0.5, neg, gs[g])
    masked = jnp.concatenate(
        [jnp.where(sel[g] > 0.5, choice[g * per_group:(g + 1) * per_group, :], neg)
         for g in range(N_GROUPS)], axis=0)
    eidx = lax.broadcasted_iota(jnp.int32, (N_EXPERTS, tm), 0).astype(F32)
    idxs, ws = [], []
    for _ in range(TOP_K):
        m = jnp.max(masked, axis=0, keepdims=True)
        i = jnp.min(jnp.where(masked == m, eidx, float(N_EXPERTS)), axis=0, keepdims=True)
        hit = eidx == i
        ws.append(jnp.sum(jnp.where(hit, scores, 0.0), axis=0, keepdims=True))
        idxs.append(i)
        masked = jnp.where(hit, neg, masked)
    wsum = functools.reduce(lambda a, b: a + b, ws)
    idx_ref[...] = jnp.concatenate(idxs, axis=0).astype(jnp.int32)
    gate_ref[...] = jnp.concatenate(ws, axis=0) / wsum * ROUTED_SCALE


def _norm_router(x, w, sc, sh, w_router, router_bias, rows, seq, n_lat, batch):
    d = x.shape[1]
    tm = _div_tile(math.gcd(seq, rows), ROW_TILE)
    rmap = _mod_row_map(tm, seq, n_lat, batch)
    return pl.pallas_call(
        _router_kernel,
        out_shape=(jax.ShapeDtypeStruct((rows, d), BF16),
                   jax.ShapeDtypeStruct((TOP_K, rows), jnp.int32),
                   jax.ShapeDtypeStruct((TOP_K, rows), F32)),
        grid=(rows // tm,),
        in_specs=[pl.BlockSpec((tm, d), lambda i: (i, 0)),
                  pl.BlockSpec((1, d), lambda i: (0, 0)),
                  pl.BlockSpec((None, 1, d), rmap),
                  pl.BlockSpec((None, 1, d), rmap),
                  pl.BlockSpec((N_EXPERTS, d), lambda i: (0, 0)),
                  pl.BlockSpec((N_EXPERTS, 1), lambda i: (0, 0))],
        out_specs=(pl.BlockSpec((tm, d), lambda i: (i, 0)),
                   pl.BlockSpec((TOP_K, tm), lambda i: (0, i)),
                   pl.BlockSpec((TOP_K, tm), lambda i: (0, i))),
        compiler_params=_cparams("parallel"),
        name="norm_router",
    )(x, w.reshape(1, d), sc, sh, w_router.T, router_bias.reshape(N_EXPERTS, 1))


def _expert_kernel(be_ref, nu_ref, x_ref, wg_ref, wu_ref, wd_ref, wt_ref, o_ref,
                   wgb_ref, wub_ref, wdb_ref):
    i = pl.program_id(0)

    @pl.when((i == 0) | (be_ref[i] != be_ref[jnp.maximum(i - 1, 0)]))
    def _():
        wgb_ref[...] = wg_ref[...].astype(BF16)
        wub_ref[...] = wu_ref[...].astype(BF16)
        wdb_ref[...] = wd_ref[...].astype(BF16)

    @pl.when(i < nu_ref[0])
    def _():
        x = x_ref[...]
        g = jnp.dot(x, wgb_ref[...], preferred_element_type=F32)
        u = jnp.dot(x, wub_ref[...], preferred_element_type=F32)
        act = (_silu(g) * u).astype(BF16)
        y = jnp.dot(act, wdb_ref[...], preferred_element_type=F32)
        o_ref[...] = (y * wt_ref[...]).astype(o_ref.dtype)

    @pl.when(i >= nu_ref[0])
    def _():
        o_ref[...] = jnp.zeros_like(o_ref)


def _experts(xs, w_gate, w_up, w_down, layer, buf_w, block_expert, n_used, bm):
    rows, d = xs.shape
    nb = rows // bm
    ff = w_gate.shape[3]
    grid_spec = pltpu.PrefetchScalarGridSpec(
        num_scalar_prefetch=2,
        grid=(nb,),
        in_specs=[pl.BlockSpec((bm, d), lambda i, be, nu: (i, 0)),
                  pl.BlockSpec((None, None, d, ff), lambda i, be, nu: (layer, be[i], 0, 0)),
                  pl.BlockSpec((None, None, d, ff), lambda i, be, nu: (layer, be[i], 0, 0)),
                  pl.BlockSpec((None, None, ff, d), lambda i, be, nu: (layer, be[i], 0, 0)),
                  pl.BlockSpec((bm, 1), lambda i, be, nu: (i, 0))],
        out_specs=pl.BlockSpec((bm, d), lambda i, be, nu: (i, 0)),
        scratch_shapes=[pltpu.VMEM((d, ff), BF16), pltpu.VMEM((d, ff), BF16),
                        pltpu.VMEM((ff, d), BF16)],
    )
    cost = pl.CostEstimate(flops=6 * rows * d * ff, transcendentals=rows * ff,
                           bytes_accessed=4 * rows * d + 12 * N_EXPERTS * d * ff)
    return pl.pallas_call(
        _expert_kernel,
        out_shape=jax.ShapeDtypeStruct((rows, d), BF16),
        grid_spec=grid_spec,
        compiler_params=_cparams("arbitrary"),
        cost_estimate=cost,
        name="routed_experts",
    )(block_expert, n_used, xs, w_gate, w_up, w_down, buf_w.reshape(rows, 1))


def _take_rows(arr, idx):
    return arr.at[idx].get(mode="promise_in_bounds")


def _dispatch_plan(idx_t, gate_t, bm):
    k, t = idx_t.shape
    a = k * t
    experts = jnp.arange(N_EXPERTS, dtype=jnp.int32)
    flat_e = idx_t.reshape(a)
    gates = gate_t.reshape(a)
    iota = jnp.arange(a, dtype=jnp.int32)
    _, order = lax.sort_key_val(flat_e, iota)
    _, inv = lax.sort_key_val(order, iota)
    counts = jnp.sum((flat_e[:, None] == experts[None, :]).astype(jnp.int32), axis=0)
    starts = jnp.cumsum(counts) - counts
    padded = (counts + bm - 1) // bm * bm
    padded_end = jnp.cumsum(padded)
    padded_start = padded_end - padded
    nb = -(-a // bm) + N_EXPERTS
    n_used = (padded_end[-1:] // bm).astype(jnp.int32)
    blk = jnp.arange(nb, dtype=jnp.int32)
    block_expert = jnp.minimum(
        jnp.sum((blk[:, None] * bm >= padded_end[None, :]).astype(jnp.int32), axis=1), N_EXPERTS - 1)
    onehot = (block_expert[:, None] == experts[None, :]).astype(jnp.int32)
    b_pstart, b_count, b_start = [jnp.sum(onehot * v[None, :], axis=1)
                                  for v in (padded_start, counts, starts)]
    rank = blk[:, None] * bm + jnp.arange(bm, dtype=jnp.int32)[None, :] - b_pstart[:, None]
    valid = (rank < b_count[:, None]).reshape(nb * bm)
    src = jnp.clip(b_start[:, None] + rank, 0, a - 1).reshape(nb * bm)
    asg = _take_rows(order, src)
    buf_tok = jnp.where(valid, asg % t, 0)
    buf_w = jnp.where(valid, _take_rows(gates, asg), 0.0)
    shift = padded_start - starts
    pos = inv + _take_rows(shift, flat_e)
    return buf_tok, buf_w, pos, block_expert.astype(jnp.int32), n_used


def _swiglu_up_kernel(a_ref, wg_ref, wu_ref, o_ref):
    a = a_ref[...]
    g = jnp.dot(a, wg_ref[...], preferred_element_type=F32)
    u = jnp.dot(a, wu_ref[...], preferred_element_type=F32)
    o_ref[...] = (_silu(g) * u).astype(o_ref.dtype)


def _swiglu_up(a, wg, wu, layer):
    m, k = a.shape
    n = wg.shape[2]
    tm = _div_tile(m, MM_TM)
    tn = _div_tile(n, 512)
    wsp = pl.BlockSpec((None, k, tn), lambda i, j: (layer, 0, j))
    return pl.pallas_call(
        _swiglu_up_kernel,
        out_shape=jax.ShapeDtypeStruct((m, n), BF16),
        grid=(m // tm, n // tn),
        in_specs=[pl.BlockSpec((tm, k), lambda i, j: (i, 0)), wsp, wsp],
        out_specs=pl.BlockSpec((tm, tn), lambda i, j: (i, j)),
        compiler_params=_cparams("parallel", "arbitrary"),
        name="shared_swiglu_up",
    )(a, wg, wu)


def _lambda_init_for(layer):
    return 0.8 - 0.6 * math.exp(-0.3 * layer)


def kernel(x, c, ctx, c_ctx, w_ada, b_ada, norm_mix, norm_ffn, w_in, hgrn_lb, hgrn_norm, diff_lambda, diff_subln, w_branch, w_mgate, b_mgate, w_out, w_router, router_bias, w_exp_gate, w_exp_up, w_exp_down, w_sh_gate, w_sh_up, w_sh_down, norm_final):
    batch, seq, d = x.shape
    ctx_len = ctx.shape[1]
    depth = w_ada.shape[0]
    n_lat, n_ctx = batch * seq, batch * ctx_len
    n_all = n_lat + n_ctx
    assert batch < 8 and seq % GRID_W == 0 and ctx_len % 8 == 0

    p_lb = jax.nn.softmax(hgrn_lb.astype(F32), axis=1)
    lower_bounds = jnp.cumsum(p_lb, axis=1) - p_lb[:, :1]

    c8 = jnp.zeros((8, d), F32).at[:batch].set(c).at[batch].set(c_ctx)
    mod = _ada_mod(c8, w_ada, b_ada)

    rope_tm = _div_tile(math.gcd(seq, n_ctx), ROW_TILE)
    rope_tabs = _rope_tables(seq, rope_tm)
    hg_tabs = _hgrn_tables(_div_tile(math.gcd(seq, ctx_len), HG_CHUNK))
    rt_tabs = _ret_tables(_div_tile(math.gcd(seq, ctx_len), RT_CHUNK))

    w_in_b, w_branch_b, w_out_b = [w.astype(BF16) for w in (w_in, w_branch, w_out)]
    w_mgate8, w_mgate_scale = _quantize_cols(w_mgate)
    w_sg_b, w_su_b, w_sd_b = [w.astype(BF16) for w in (w_sh_gate, w_sh_up, w_sh_down)]

    xa = jnp.concatenate([x.reshape(n_lat, d), ctx.reshape(n_ctx, d)], axis=0)
    for l in range(depth):
        need_ctx = l < depth - 1
        rows = n_all if need_ctx else n_lat
        sh1, sc1, g1, sh2, sc2, g2 = [mod[l, :, i * d:(i + 1) * d].reshape(8, 1, d) for i in range(6)]

        h, h8, h_scale = _norm_mod(xa, norm_mix[l], sc1, sh1, n_all, seq, n_lat, batch, BF16,
                                   quantize=True)
        p = _matmul(h, w_in_b, l, BF16)
        qd, kd, qr, kr = _rope_prep(p, rope_tabs, rope_tm, seq, n_lat)
        y_hg, yc_hg = _hgrn(p, lower_bounds[0, l], lower_bounds[1, l], hgrn_norm[l], hg_tabs,
                            batch, seq, ctx_len, need_ctx)
        lam_init = _lambda_init_for(l)
        y_da = _diff_attn(qd, kd, p, diff_lambda[l], diff_subln[l], lam_init, batch, seq, ctx_len, True)
        y_rt, yc_rt = _retention(qr, kr, p, rt_tabs, batch, seq, ctx_len, need_ctx)
        ys = [y_hg, y_da, y_rt]
        if need_ctx:
            yc_da = _diff_attn(qd, kd, p, diff_lambda[l], diff_subln[l], lam_init, batch, seq,
                               ctx_len, False)
            ys = [jnp.concatenate([a, b], axis=0) for a, b in zip(ys, (yc_hg, yc_da, yc_rt))]
        merged = _merge(h8, h_scale, ys, w_mgate8, w_mgate_scale, b_mgate, w_branch_b, l, rows)
        tm = _proj_tile(rows, seq)
        n_tiles = rows // tm
        xa = _proj_residual(merged, w_out_b, l, xa, g1, None, 0, n_tiles, tm, seq, n_lat, batch)

        h2, idx_t, gate_t = _norm_router(xa, norm_ffn[l], sc2, sh2, w_router[l], router_bias[l],
                                         rows, seq, n_lat, batch)
        up = _swiglu_up(h2, w_sg_b, w_su_b, l)
        bounds = [n_tiles * g // MOE_GROUPS for g in range(MOE_GROUPS + 1)]
        groups = [(t0, t1 - t0) for t0, t1 in zip(bounds[:-1], bounds[1:]) if t1 > t0]
        plans, gathered, y_slots = [], [], []
        for t0, nt in groups:
            cols = slice(t0 * tm, (t0 + nt) * tm)
            plans.append(_dispatch_plan(idx_t[:, cols], gate_t[:, cols], MOE_BM))
        for (t0, nt), plan in zip(groups, plans):
            gathered.append(_take_rows(h2, plan[0] + t0 * tm))
        for (t0, nt), plan, xs in zip(groups, plans, gathered):
            buf_tok, buf_w, pos, block_expert, n_used = plan
            y_sorted = _experts(xs, w_exp_gate, w_exp_up, w_exp_down, l, buf_w, block_expert,
                                n_used, MOE_BM)
            y_slots.append(_take_rows(y_sorted, pos).reshape(TOP_K, nt * tm, d))
        parts = [_proj_residual(up, w_sd_b, l, xa, g2, ys_g, t0, nt, tm, seq, n_lat, batch)
                 for (t0, nt), ys_g in zip(groups, y_slots)]
        xa = parts[0] if len(parts) == 1 else jnp.concatenate(parts, axis=0)

    zeros = jnp.zeros((8, 1, d), F32)
    out = _norm_mod(xa, norm_final, zeros, zeros, n_lat, seq, n_lat, batch, F32)
    return out.reshape(batch, seq, d)
```

```python
import functools
import math

import numpy as np
import jax
import jax.numpy as jnp
from jax import lax
from jax.experimental import pallas as pl
from jax.experimental.pallas import tpu as pltpu

F32 = jnp.float32
BF16 = jnp.bfloat16
FP8 = jnp.float8_e4m3fn
FP8_MAX = 448.0

NORM_EPS = 1e-6
ROPE_BASE = 10000.0
GRID_W = 64
HG_HEADS = 8
DA_HEADS = 8
DA_DH = 64
RT_HEADS = 8
RT_DK = 64
N_EXPERTS = 64
TOP_K = 8
N_GROUPS = 8
TOPK_GROUPS = 4
EXPERT_FF = 256
ROUTED_SCALE = 2.5
BRANCH_W = 1024

LANES = 128
V7X_VMEM_BYTES = 64 * 1024 * 1024
VMEM_LIMIT = 56 * 1024 * 1024

COL_HG_Q, COL_HG_FF, COL_HG_FB, COL_HG_I, COL_HG_G = 0, 8, 16, 24, 32
COL_DA_Q, COL_DA_K, COL_DA_V = 40, 48, 56
COL_RT_Q, COL_RT_K, COL_RT_V, COL_RT_G = 64, 68, 72, 80

NT_DIMS = (((1,), (1,)), ((), ()))
TN_DIMS = (((0,), (0,)), ((), ()))

ROW_TILE = 256
MM_TM = 1024
HG_CHUNK = 128
RT_CHUNK = 256
ATT_TQ = 512
ATT_TK = 512
ATT_VROWS = LANES + 16
MOE_BM = 256
MOE_GROUPS = 1


def _cparams(*sem):
    return pltpu.CompilerParams(dimension_semantics=sem, vmem_limit_bytes=VMEM_LIMIT)


def _div_tile(n, pref):
    t = min(n, pref)
    while n % t:
        t //= 2
    return t


def _sigmoid(x):
    return 1.0 / (1.0 + jnp.exp(-x))


def _silu(x):
    return x * _sigmoid(x)


def _ada_kernel(c_ref, w_ref, b_ref, o_ref):
    a = _silu(c_ref[...]).astype(BF16)
    o_ref[...] = jnp.dot(a, w_ref[...].astype(BF16), preferred_element_type=F32) + b_ref[...]


def _ada_mod(c8, w_ada, b_ada):
    depth, d, n = w_ada.shape
    tn = _div_tile(n, 512)
    return pl.pallas_call(
        _ada_kernel,
        out_shape=jax.ShapeDtypeStruct((depth, 8, n), F32),
        grid=(depth, n // tn),
        in_specs=[
            pl.BlockSpec((8, d), lambda l, j: (0, 0)),
            pl.BlockSpec((None, d, tn), lambda l, j: (l, 0, j)),
            pl.BlockSpec((None, 1, tn), lambda l, j: (l, 0, j)),
        ],
        out_specs=pl.BlockSpec((None, 8, tn), lambda l, j: (l, 0, j)),
        compiler_params=_cparams("arbitrary", "arbitrary"),
        name="ada_mod",
    )(c8, w_ada, b_ada.reshape(depth, 1, n))


def _mod_row_map(tm, seq, n_lat, batch):
    def index_map(i):
        start = i * tm
        return (jnp.where(start < n_lat, start // seq, batch), 0, 0)
    return index_map


def _norm_mod_kernel(quantize, x_ref, w_ref, sc_ref, sh_ref, o_ref, *q_refs):
    x = x_ref[...]
    ms = jnp.mean(x * x, axis=-1, keepdims=True)
    y = x * lax.rsqrt(ms + NORM_EPS) * w_ref[...]
    h = y * (1.0 + sc_ref[...]) + sh_ref[...]
    o_ref[...] = h.astype(o_ref.dtype)
    if quantize:
        q_ref, s_ref = q_refs
        amax = jnp.max(jnp.abs(h), axis=-1, keepdims=True)
        scale = jnp.where(amax > 0.0, amax * (1.0 / FP8_MAX), 1.0)
        q_ref[...] = (h / scale).astype(FP8)
        s_ref[...] = scale


def _norm_mod(x, w, sc, sh, rows, seq, n_lat, batch, out_dtype, quantize=False):
    d = x.shape[1]
    tm = _div_tile(math.gcd(seq, rows), ROW_TILE)
    rmap = _mod_row_map(tm, seq, n_lat, batch)
    row_blk = pl.BlockSpec((tm, d), lambda i: (i, 0))
    out_shape = [jax.ShapeDtypeStruct((rows, d), out_dtype)]
    out_specs = [row_blk]
    if quantize:
        out_shape += [jax.ShapeDtypeStruct((rows, d), FP8), jax.ShapeDtypeStruct((rows, 1), F32)]
        out_specs += [row_blk, pl.BlockSpec((tm, 1), lambda i: (i, 0))]
    outs = pl.pallas_call(
        functools.partial(_norm_mod_kernel, quantize),
        out_shape=tuple(out_shape),
        grid=(rows // tm,),
        in_specs=[
            row_blk,
            pl.BlockSpec((1, d), lambda i: (0, 0)),
            pl.BlockSpec((None, 1, d), rmap),
            pl.BlockSpec((None, 1, d), rmap),
        ],
        out_specs=tuple(out_specs),
        compiler_params=_cparams("parallel"),
        name="norm_mod",
    )(x, w.reshape(1, d), sc, sh)
    return outs if quantize else outs[0]


def _mm_kernel(a_ref, b_ref, o_ref):
    o_ref[...] = jnp.dot(a_ref[...], b_ref[...], preferred_element_type=F32).astype(o_ref.dtype)


def _matmul(a, b, layer, out_dtype, tn_pref=1024):
    m, k = a.shape
    n = b.shape[2]
    tm = _div_tile(m, MM_TM)
    tn = _div_tile(n, tn_pref)
    return pl.pallas_call(
        _mm_kernel,
        out_shape=jax.ShapeDtypeStruct((m, n), out_dtype),
        grid=(m // tm, n // tn),
        in_specs=[pl.BlockSpec((tm, k), lambda i, j: (i, 0)),
                  pl.BlockSpec((None, k, tn), lambda i, j: (layer, 0, j))],
        out_specs=pl.BlockSpec((tm, tn), lambda i, j: (i, j)),
        compiler_params=_cparams("parallel", "arbitrary"),
        name="matmul",
    )(a, b)


def _rope_tables(seq, n_ctx):
    pos = np.arange(seq, dtype=np.float64)
    rows = np.floor(pos / GRID_W)
    cols = pos - rows * GRID_W
    lane = np.arange(LANES)
    j = lane % 64
    part, jj = j // 32, j % 32
    inv = ROPE_BASE ** (-np.arange(0, 32, 2, dtype=np.float64) / 32.0)
    ang = np.where(part[None, :] == 0, rows[:, None], cols[:, None]) * inv[jj % 16][None, :]
    cos_da = np.cos(ang)
    sin_da = np.sin(ang) * np.where(jj < 16, -1.0, 1.0)[None, :]
    partner_da = np.where(jj < 16, lane + 16, lane - 16)
    theta = 1.0 / (ROPE_BASE ** np.linspace(0.0, 1.0, 32))
    ang = pos[:, None] * theta[j % 32][None, :]
    cos_rt = np.cos(ang)
    sin_rt = np.sin(ang) * np.where(j < 32, -1.0, 1.0)[None, :]
    partner_rt = np.where(j < 32, lane + 32, lane - 32)

    def full(tab, fill):
        return jnp.asarray(np.concatenate([tab, np.full((n_ctx, LANES), fill)], 0), F32)

    def perm(partner):
        p = np.zeros((LANES, LANES), np.float32)
        p[partner, lane] = 1.0
        return jnp.asarray(p, BF16)

    return (full(cos_da, 1.0), full(sin_da, 0.0), full(cos_rt, 1.0), full(sin_rt, 0.0),
            perm(partner_da), perm(partner_rt))


def _rope_kernel(qd_ref, kd_ref, qr_ref, kr_ref, cd_ref, sd_ref, cr_ref, sr_ref, pd_ref, pr_ref,
                 oqd_ref, okd_ref, oqr_ref, okr_ref):
    def rot(src, dst, cos_ref, sin_ref, perm_ref, scale):
        cos = cos_ref[...]
        sin = sin_ref[...]
        perm = perm_ref[...]
        for g in range(src.shape[1] // LANES):
            sl = slice(g * LANES, (g + 1) * LANES)
            x = src[:, sl]
            xs = jnp.dot(x, perm, preferred_element_type=F32)
            y = x.astype(F32) * cos + xs * sin
            dst[:, sl] = (y * scale).astype(dst.dtype)

    rot(qd_ref, oqd_ref, cd_ref, sd_ref, pd_ref, DA_DH ** -0.5 * math.log2(math.e))
    rot(kd_ref, okd_ref, cd_ref, sd_ref, pd_ref, 1.0)
    rot(qr_ref, oqr_ref, cr_ref, sr_ref, pr_ref, 1.0)
    rot(kr_ref, okr_ref, cr_ref, sr_ref, pr_ref, RT_DK ** -0.5)


def _rope_prep(p, tables, tm, seq, n_lat):
    rows = p.shape[0]
    cos_da, sin_da, cos_rt, sin_rt, perm_da, perm_rt = tables
    lat_tiles, seq_tiles = n_lat // tm, seq // tm

    def tmap(i):
        return (jnp.where(i < lat_tiles, i % seq_tiles, seq_tiles), 0)

    wd, wr = DA_HEADS * 2 * DA_DH, RT_HEADS * RT_DK
    tab = pl.BlockSpec((tm, LANES), tmap)
    pm = pl.BlockSpec((LANES, LANES), lambda i: (0, 0))
    return pl.pallas_call(
        _rope_kernel,
        out_shape=(jax.ShapeDtypeStruct((rows, wd), BF16), jax.ShapeDtypeStruct((rows, wd), BF16),
                   jax.ShapeDtypeStruct((rows, wr), BF16), jax.ShapeDtypeStruct((rows, wr), BF16)),
        grid=(rows // tm,),
        in_specs=[
            pl.BlockSpec((tm, wd), lambda i: (i, COL_DA_Q * LANES // wd)),
            pl.BlockSpec((tm, wd), lambda i: (i, COL_DA_K * LANES // wd)),
            pl.BlockSpec((tm, wr), lambda i: (i, COL_RT_Q * LANES // wr)),
            pl.BlockSpec((tm, wr), lambda i: (i, COL_RT_K * LANES // wr)),
            tab, tab, tab, tab, pm, pm,
        ],
        out_specs=(pl.BlockSpec((tm, wd), lambda i: (i, 0)), pl.BlockSpec((tm, wd), lambda i: (i, 0)),
                   pl.BlockSpec((tm, wr), lambda i: (i, 0)), pl.BlockSpec((tm, wr), lambda i: (i, 0))),
        compiler_params=_cparams("parallel"),
        name="rope_prep",
    )(p, p, p, p, cos_da, sin_da, cos_rt, sin_rt, perm_da, perm_rt)


def _hgrn_tables(c):
    nlev = int(math.log2(c))
    m = np.zeros((nlev + 2, c, c), np.float32)
    w = np.zeros((nlev + 1, c, c), np.float32)
    for l in range(nlev):
        bs = c >> (l + 1)
        for t in range(c):
            blk = t // bs
            if blk % 2 == 1:
                m[l, t, blk * bs:t + 1] = 1.0
                w[l, t, (blk - 1) * bs:blk * bs] = 1.0
            else:
                m[l, t, t + 1:(blk + 1) * bs] = 1.0
    for t in range(c):
        m[nlev, t, :t + 1] = 1.0
        m[nlev + 1, t, t + 1:] = 1.0
    w[nlev] = np.eye(c)
    mb = m[:, ::-1, ::-1]
    wb = w[:, ::-1, ::-1]
    r = (nlev + 2) * c
    return (jnp.asarray(m.reshape(r, c), BF16), jnp.asarray(mb.reshape(r, c), BF16),
            jnp.asarray(w, F32), jnp.asarray(wb, F32))


def _hgrn_chunk(q_raw, f_raw, v, lb, m_ref, w_ref, st, fwd):
    c = q_raw.shape[0]
    nlev = w_ref.shape[0] - 1
    q = _silu(q_raw.astype(F32))
    f = lb + (1.0 - lb) * _sigmoid(f_raw.astype(F32))
    lf = jnp.log(f)
    k = 1.0 - f
    hi = lf.astype(BF16)
    lo = (lf - hi.astype(F32)).astype(BF16)
    g2 = jnp.dot(m_ref[...], jnp.concatenate([hi, lo], axis=1), preferred_element_type=F32)
    g = g2[:, :LANES] + g2[:, LANES:]
    e = jnp.exp(g)
    row = lax.broadcasted_iota(jnp.int32, (c, LANES), 0)
    a = w_ref[nlev] * lax.dot_general(q.astype(BF16), k.astype(BF16), NT_DIMS,
                                      preferred_element_type=F32)
    for l in range(nlev):
        shift = int(math.log2(c >> (l + 1)))
        odd = ((row >> shift) & 1) == 1
        x = (jnp.where(odd, q, k) if fwd else jnp.where(odd, k, q)) * e[l * c:(l + 1) * c]
        xb = x.astype(BF16)
        a = a + w_ref[l] * lax.dot_general(xb, xb, NT_DIMS, preferred_element_type=F32)
    qe = (q * e[nlev * c:(nlev + 1) * c]).astype(BF16)
    ke = (k * e[(nlev + 1) * c:(nlev + 2) * c]).astype(BF16)
    o = (jnp.dot(a.astype(BF16), v, preferred_element_type=F32)
         + lax.dot_general(qe, st.astype(BF16), NT_DIMS, preferred_element_type=F32))
    last = nlev * c + (c - 1 if fwd else 0)
    dec = jnp.exp(g[last:last + 1, :])
    st_new = dec * st + lax.dot_general(v, ke, TN_DIMS, preferred_element_type=F32)
    return o, st_new


def _hgrn_kernel(need_ctx, chunk, *refs):
    (ql, ffl, fbl, il, gl, qc, ffc, fbc, ic, gc, lbf_ref, lbb_ref, nw_ref,
     mf_ref, mb_ref, wf_ref, wb_ref) = refs[:17]
    rest = refs[17:]
    if need_ctx:
        yl_ref, yc_ref, of_ref, ob_ref, ocf_ref, ocb_ref, st_ref = rest
    else:
        yl_ref, of_ref, ob_ref, st_ref = rest
        yc_ref = ocf_ref = ocb_ref = None
    nc_lat = ql.shape[0] // chunk
    nc_ctx = qc.shape[0] // chunk
    lbf = lbf_ref[...]
    lbb = lbb_ref[...]
    st_ref[...] = jnp.zeros_like(st_ref)

    def step(j, q_ref, ff_ref, fb_ref, i_ref, n, outf, outb):
        sf = pl.ds(pl.multiple_of(j * chunk, chunk), chunk)
        sb = pl.ds(pl.multiple_of((n - 1 - j) * chunk, chunk), chunk)
        o, s = _hgrn_chunk(q_ref[sf, :], ff_ref[sf, :], i_ref[sf, :], lbf, mf_ref, wf_ref,
                           st_ref[0], True)
        st_ref[0] = s
        if outf is not None:
            outf[sf, :] = o
        o, s = _hgrn_chunk(q_ref[sb, :], fb_ref[sb, :], i_ref[sb, :], lbb, mb_ref, wb_ref,
                           st_ref[1], False)
        st_ref[1] = s
        if outb is not None:
            outb[sb, :] = o

    def ctx_body(j, carry):
        step(j, qc, ffc, fbc, ic, nc_ctx, ocf_ref, ocb_ref)
        return carry

    def lat_body(j, carry):
        step(j, ql, ffl, fbl, il, nc_lat, of_ref, ob_ref)
        return carry

    lax.fori_loop(0, nc_ctx, ctx_body, 0)
    lax.fori_loop(0, nc_lat, lat_body, 0, unroll=2)

    nw = nw_ref[...]

    def readout(n, a_ref, b_ref, g_ref, y_ref):
        def body(j, carry):
            s = pl.ds(pl.multiple_of(j * chunk, chunk), chunk)
            o = a_ref[s, :] + b_ref[s, :]
            o = o * lax.rsqrt(jnp.mean(o * o, axis=-1, keepdims=True) + NORM_EPS) * nw
            y_ref[s, :] = (o * _silu(g_ref[s, :].astype(F32))).astype(y_ref.dtype)
            return carry
        lax.fori_loop(0, n, body, 0)

    readout(nc_lat, of_ref, ob_ref, gl, yl_ref)
    if need_ctx:
        readout(nc_ctx, ocf_ref, ocb_ref, gc, yc_ref)


def _hgrn(p, lb_f, lb_b, norm_w, tables, batch, seq, ctx_len, need_ctx):
    n_lat = batch * seq
    chunk = _div_tile(math.gcd(seq, ctx_len), HG_CHUNK)
    mf, mb, wf, wb = tables
    ctx_blk0 = n_lat // ctx_len
    width = HG_HEADS * LANES

    def lat(col):
        return pl.BlockSpec((seq, LANES), lambda b, h: (b, col + h))

    def ctx(col):
        return pl.BlockSpec((ctx_len, LANES), lambda b, h: (ctx_blk0 + b, col + h))

    head_vec = pl.BlockSpec((1, LANES), lambda b, h: (0, h))
    const2 = lambda arr: pl.BlockSpec(arr.shape, lambda b, h: (0,) * arr.ndim)
    cols = (COL_HG_Q, COL_HG_FF, COL_HG_FB, COL_HG_I, COL_HG_G)
    in_specs = ([lat(c) for c in cols] + [ctx(c) for c in cols]
                + [head_vec, head_vec, pl.BlockSpec((1, LANES), lambda b, h: (0, 0)),
                   const2(mf), const2(mb), const2(wf), const2(wb)])
    out_shape = [jax.ShapeDtypeStruct((n_lat, width), BF16)]
    out_specs = [pl.BlockSpec((seq, LANES), lambda b, h: (b, h))]
    scratch = [pltpu.VMEM((seq, LANES), F32), pltpu.VMEM((seq, LANES), F32)]
    if need_ctx:
        out_shape.append(jax.ShapeDtypeStruct((batch * ctx_len, width), BF16))
        out_specs.append(pl.BlockSpec((ctx_len, LANES), lambda b, h: (b, h)))
        scratch += [pltpu.VMEM((ctx_len, LANES), F32), pltpu.VMEM((ctx_len, LANES), F32)]
    scratch.append(pltpu.VMEM((2, LANES, LANES), F32))
    outs = pl.pallas_call(
        functools.partial(_hgrn_kernel, need_ctx, chunk),
        out_shape=tuple(out_shape),
        grid=(batch, HG_HEADS),
        in_specs=in_specs,
        out_specs=tuple(out_specs),
        scratch_shapes=scratch,
        compiler_params=_cparams("parallel", "parallel"),
        name="hgrn2_scan",
    )(*([p] * 10), lb_f.reshape(1, width), lb_b.reshape(1, width), norm_w.reshape(1, LANES),
      mf, mb, wf, wb)
    return outs[0], (outs[1] if need_ctx else None)


def _attn_kernel(has_lat, tk, lambda_init, *refs):
    if has_lat:
        (q_ref, kc_ref, vc_ref, kl_ref, vl_ref, lam_ref, sub_ref, o_ref,
         s_ref, vtc_ref, vtl_ref) = refs
    else:
        q_ref, kc_ref, vc_ref, lam_ref, sub_ref, o_ref, s_ref, vtc_ref = refs
    tq = q_ref.shape[0]
    w2 = 2 * tq
    nctx = kc_ref.shape[0]
    nlat = kl_ref.shape[0] if has_lat else 0
    n_chunks = nlat // tk
    sub_t = min(256, tk)

    @pl.when(pl.program_id(2) == 0)
    def _():
        vtc_ref[LANES:, :] = jnp.ones((ATT_VROWS - LANES, nctx), BF16)
        for r in range(0, nctx, min(sub_t, nctx)):
            rr = slice(r, r + min(sub_t, nctx))
            vtc_ref[:LANES, rr] = vc_ref[rr, :].astype(F32).T.astype(BF16)
        for j in range(n_chunks):
            vtl_ref[j, LANES:, :] = jnp.ones((ATT_VROWS - LANES, tk), BF16)
            for r in range(0, tk, sub_t):
                vtl_ref[j, :LANES, r:r + sub_t] = (
                    vl_ref[j * tk + r:j * tk + r + sub_t, :].astype(F32).T.astype(BF16))

    q = q_ref[...].astype(F32).T
    sub = lax.broadcasted_iota(jnp.int32, (LANES, tq), 0)
    qst = jnp.concatenate([jnp.where(sub < DA_DH, q, 0.0), jnp.where(sub >= DA_DH, q, 0.0)],
                          axis=1).astype(BF16)

    def col_fold(x, op):
        return op(x.reshape(x.shape[0] // 8, 8, w2), axis=0)

    def scores(k_ref, src, dst, macc):
        s = jnp.dot(k_ref[src, :], qst, preferred_element_type=F32)
        s_ref[dst, :] = s
        return jnp.maximum(macc, col_fold(s, jnp.max))

    macc = scores(kc_ref, slice(0, nctx), slice(0, nctx), jnp.full((8, w2), -jnp.inf, F32))
    for j in range(n_chunks):
        macc = scores(kl_ref, slice(j * tk, (j + 1) * tk), slice(nctx + j * tk, nctx + (j + 1) * tk),
                      macc)
    m = jnp.max(macc, axis=0, keepdims=True)

    def weights(rows):
        return jnp.exp2((s_ref[rows, :] - m).astype(BF16))

    pv = jnp.dot(vtc_ref[...], weights(slice(0, nctx)), preferred_element_type=F32)
    for j in range(n_chunks):
        pv = pv + jnp.dot(vtl_ref[j], weights(slice(nctx + j * tk, nctx + (j + 1) * tk)),
                          preferred_element_type=F32)
    on = pv[:LANES] / pv[LANES:LANES + 1]
    lv = lam_ref[...]
    lam = (jnp.exp(jnp.sum(lv[0:1] * lv[1:2], axis=-1, keepdims=True))
           - jnp.exp(jnp.sum(lv[2:3] * lv[3:4], axis=-1, keepdims=True)) + lambda_init)
    o = on[:, :tq] - lam * on[:, tq:]
    o = o * lax.rsqrt(jnp.mean(o * o, axis=0, keepdims=True) + NORM_EPS) * sub_ref[...]
    o_ref[...] = (o * (1.0 - lambda_init)).T.astype(o_ref.dtype)


def _diff_attn(qd, kd, p, lam_vec, subln_w, lambda_init, batch, seq, ctx_len, latent_queries):
    n_lat = batch * seq
    ctx_blk0 = n_lat // ctx_len
    width = DA_HEADS * LANES
    lq = seq if latent_queries else ctx_len
    tq = _div_tile(lq, ATT_TQ)
    q_blk0 = 0 if latent_queries else n_lat // tq
    qt = lq // tq
    tk = _div_tile(seq, ATT_TK)
    in_specs = [
        pl.BlockSpec((tq, LANES), lambda b, h, i: (q_blk0 + b * qt + i, h)),
        pl.BlockSpec((ctx_len, LANES), lambda b, h, i: (ctx_blk0 + b, h)),
        pl.BlockSpec((ctx_len, LANES), lambda b, h, i: (ctx_blk0 + b, COL_DA_V + h)),
    ]
    args = [qd, kd, p]
    n_keys = ctx_len + (seq if latent_queries else 0)
    scratch = [pltpu.VMEM((n_keys, 2 * tq), F32), pltpu.VMEM((ATT_VROWS, ctx_len), BF16)]
    if latent_queries:
        in_specs += [pl.BlockSpec((seq, LANES), lambda b, h, i: (b, h)),
                     pl.BlockSpec((seq, LANES), lambda b, h, i: (b, COL_DA_V + h))]
        args += [kd, p]
        scratch.append(pltpu.VMEM((seq // tk, ATT_VROWS, tk), BF16))
    in_specs += [pl.BlockSpec((4, DA_DH), lambda b, h, i: (0, 0)),
                 pl.BlockSpec((LANES, 1), lambda b, h, i: (0, 0))]
    args += [lam_vec, subln_w.reshape(LANES, 1)]
    return pl.pallas_call(
        functools.partial(_attn_kernel, latent_queries, tk, lambda_init),
        out_shape=jax.ShapeDtypeStruct((batch * lq, width), BF16),
        grid=(batch, DA_HEADS, qt),
        in_specs=in_specs,
        out_specs=pl.BlockSpec((tq, LANES), lambda b, h, i: (b * qt + i, h)),
        scratch_shapes=scratch,
        compiler_params=_cparams("parallel", "parallel", "arbitrary"),
        name="diff_attn_lat" if latent_queries else "diff_attn_ctx",
    )(*args)


def _ret_tables(c):
    gam = 1.0 - 2.0 ** (-5.0 - np.arange(RT_HEADS, dtype=np.float64))
    idx = np.arange(c, dtype=np.float64)
    dist = np.abs(idx[:, None] - idx[None, :])
    dsym = gam[:, None, None] ** dist[None] * np.where(dist == 0, 2.0, 1.0)[None]
    lane_head = np.arange(LANES) // RT_DK
    mask = (lane_head[None, :] == (np.arange(RT_HEADS) % 2)[:, None]).astype(np.float64)

    def tab(power):
        return (gam[:, None] ** power[None, :])[:, :, None] * mask[:, None, :]

    qdf, kdf = tab(idx + 1.0), tab(c - 1.0 - idx)
    qdb, kdb = tab(c - idx), tab(idx)
    cdec = np.broadcast_to((gam ** c)[:, None, None], (RT_HEADS, 1, LANES))
    hp = RT_HEADS // 2
    f = lambda x: jnp.asarray(np.ascontiguousarray(x).reshape((hp, 2) + x.shape[1:]), F32)
    return f(dsym), f(mask[:, None, :]), f(qdf), f(kdf), f(qdb), f(kdb), f(cdec)


def _ret_kernel(need_ctx, chunk, *refs):
    (ql, kl, vl, gl, qc, kc, vc, gc, dsym_ref, msk_ref, qdf_ref, kdf_ref, qdb_ref, kdb_ref,
     cdec_ref) = refs[:15]
    rest = refs[15:]
    if need_ctx:
        yl_ref, yc_ref, of_ref, ob_ref, ocf_ref, ocb_ref, st_ref = rest
    else:
        yl_ref, of_ref, ob_ref, st_ref = rest
        yc_ref = ocf_ref = ocb_ref = None
    nc_lat = ql.shape[0] // chunk
    nc_ctx = qc.shape[0] // chunk
    st_ref[...] = jnp.zeros_like(st_ref)

    def step(j, q_ref, k_ref, v_ref, n, outf, outb):
        sf = pl.ds(pl.multiple_of(j * chunk, chunk), chunk)
        sb = pl.ds(pl.multiple_of((n - 1 - j) * chunk, chunk), chunk)
        qf = q_ref[sf, :].astype(F32)
        kfb = k_ref[sf, :]
        kf = kfb.astype(F32)
        qb = q_ref[sb, :].astype(F32)
        kb = k_ref[sb, :].astype(F32)
        for h in range(2):
            hs = slice(h * LANES, (h + 1) * LANES)
            vf = v_ref[sf, hs]
            vb = v_ref[sb, hs]
            a = lax.dot_general((qf * msk_ref[h]).astype(BF16), kfb, NT_DIMS,
                                preferred_element_type=F32) * dsym_ref[h]
            o = (jnp.dot(a.astype(BF16), vf, preferred_element_type=F32)
                 + jnp.dot((qf * qdf_ref[h]).astype(BF16), st_ref[h].astype(BF16),
                           preferred_element_type=F32))
            st_ref[h] = cdec_ref[h] * st_ref[h] + lax.dot_general(
                (kf * kdf_ref[h]).astype(BF16), vf, TN_DIMS, preferred_element_type=F32)
            if outf is not None:
                outf[sf, hs] = o
            o = jnp.dot((qb * qdb_ref[h]).astype(BF16), st_ref[2 + h].astype(BF16),
                        preferred_element_type=F32)
            st_ref[2 + h] = cdec_ref[h] * st_ref[2 + h] + lax.dot_general(
                (kb * kdb_ref[h]).astype(BF16), vb, TN_DIMS, preferred_element_type=F32)
            if outb is not None:
                outb[sb, hs] = o

    def ctx_body(j, carry):
        step(j, qc, kc, vc, nc_ctx, ocf_ref, ocb_ref)
        return carry

    def lat_body(j, carry):
        step(j, ql, kl, vl, nc_lat, of_ref, ob_ref)
        return carry

    lax.fori_loop(0, nc_ctx, ctx_body, 0)
    lax.fori_loop(0, nc_lat, lat_body, 0)

    def readout(n, a_ref, b_ref, g_ref, y_ref):
        def body(j, carry):
            s = pl.ds(pl.multiple_of(j * chunk, chunk), chunk)
            for h in range(2):
                hs = slice(h * LANES, (h + 1) * LANES)
                o = a_ref[s, hs] + b_ref[s, hs]
                o = o * lax.rsqrt(jnp.mean(o * o, axis=-1, keepdims=True) + NORM_EPS)
                y_ref[s, hs] = (o * _silu(g_ref[s, hs].astype(F32))).astype(y_ref.dtype)
            return carry
        lax.fori_loop(0, n, body, 0)

    readout(nc_lat, of_ref, ob_ref, gl, yl_ref)
    if need_ctx:
        readout(nc_ctx, ocf_ref, ocb_ref, gc, yc_ref)


def _retention(qr, kr, p, tables, batch, seq, ctx_len, need_ctx):
    n_lat = batch * seq
    chunk = _div_tile(math.gcd(seq, ctx_len), RT_CHUNK)
    ctx_blk0 = n_lat // ctx_len
    hp = RT_HEADS // 2
    pair = 2 * LANES
    width = RT_HEADS * LANES
    v_col, g_col = COL_RT_V * LANES // pair, COL_RT_G * LANES // pair

    in_specs = [
        pl.BlockSpec((seq, LANES), lambda b, h: (b, h)),
        pl.BlockSpec((seq, LANES), lambda b, h: (b, h)),
        pl.BlockSpec((seq, pair), lambda b, h: (b, v_col + h)),
        pl.BlockSpec((seq, pair), lambda b, h: (b, g_col + h)),
        pl.BlockSpec((ctx_len, LANES), lambda b, h: (ctx_blk0 + b, h)),
        pl.BlockSpec((ctx_len, LANES), lambda b, h: (ctx_blk0 + b, h)),
        pl.BlockSpec((ctx_len, pair), lambda b, h: (ctx_blk0 + b, v_col + h)),
        pl.BlockSpec((ctx_len, pair), lambda b, h: (ctx_blk0 + b, g_col + h)),
    ]
    for t in tables:
        in_specs.append(pl.BlockSpec((None,) + t.shape[1:], lambda b, h: (h, 0, 0, 0)))
    out_shape = [jax.ShapeDtypeStruct((n_lat, width), BF16)]
    out_specs = [pl.BlockSpec((seq, pair), lambda b, h: (b, h))]
    scratch = [pltpu.VMEM((seq, pair), F32), pltpu.VMEM((seq, pair), F32)]
    if need_ctx:
        out_shape.append(jax.ShapeDtypeStruct((batch * ctx_len, width), BF16))
        out_specs.append(pl.BlockSpec((ctx_len, pair), lambda b, h: (b, h)))
        scratch += [pltpu.VMEM((ctx_len, pair), F32), pltpu.VMEM((ctx_len, pair), F32)]
    scratch.append(pltpu.VMEM((4, LANES, LANES), F32))
    outs = pl.pallas_call(
        functools.partial(_ret_kernel, need_ctx, chunk),
        out_shape=tuple(out_shape),
        grid=(batch, hp),
        in_specs=in_specs,
        out_specs=tuple(out_specs),
        scratch_shapes=scratch,
        compiler_params=_cparams("parallel", "parallel"),
        name="retention_scan",
    )(qr, kr, p, p, qr, kr, p, p, *tables)
    return outs[0], (outs[1] if need_ctx else None)


def _merge_kernel(h_ref, hs_ref, y0_ref, y1_ref, y2_ref, wg_ref, ws_ref, bg_ref, wb_ref, o_ref):
    h = h_ref[...]
    hs = hs_ref[...]
    acc = None
    for i, y_ref in enumerate((y0_ref, y1_ref, y2_ref)):
        logits = jnp.dot(h, wg_ref[i], preferred_element_type=F32) * hs * ws_ref[i] + bg_ref[i]
        t = _sigmoid(logits) * jnp.dot(y_ref[...], wb_ref[i], preferred_element_type=F32)
        acc = t if acc is None else acc + t
    o_ref[...] = acc.astype(o_ref.dtype)


def _quantize_cols(w):
    amax = jnp.max(jnp.abs(w), axis=-2, keepdims=True)
    scale = jnp.where(amax > 0.0, amax * (1.0 / FP8_MAX), 1.0)
    return (w / scale).astype(FP8), scale


def _merge(h8, h_scale, ys, w_mgate8, w_mgate_scale, b_mgate, w_branch, layer, rows):
    d = h8.shape[1]
    tm = _div_tile(rows, MM_TM)
    tn = _div_tile(d, 512)
    ysp = pl.BlockSpec((tm, BRANCH_W), lambda i, j: (i, 0))
    vec = pl.BlockSpec((None, 3, 1, tn), lambda i, j: (layer, 0, 0, j))
    return pl.pallas_call(
        _merge_kernel,
        out_shape=jax.ShapeDtypeStruct((rows, d), BF16),
        grid=(rows // tm, d // tn),
        in_specs=[pl.BlockSpec((tm, d), lambda i, j: (i, 0)),
                  pl.BlockSpec((tm, 1), lambda i, j: (i, 0)), ysp, ysp, ysp,
                  pl.BlockSpec((None, 3, d, tn), lambda i, j: (layer, 0, 0, j)), vec, vec,
                  pl.BlockSpec((None, 3, BRANCH_W, tn), lambda i, j: (layer, 0, 0, j))],
        out_specs=pl.BlockSpec((tm, tn), lambda i, j: (i, j)),
        compiler_params=_cparams("parallel", "arbitrary"),
        name="branch_merge",
    )(h8, h_scale, *ys, w_mgate8, w_mgate_scale, b_mgate.reshape(b_mgate.shape[0], 3, 1, d),
      w_branch)


def _proj_res_kernel(has_extra, *refs):
    if has_extra:
        a_ref, w_ref, x_ref, g_ref, e_ref, o_ref = refs
    else:
        a_ref, w_ref, x_ref, g_ref, o_ref = refs
    y = jnp.dot(a_ref[...], w_ref[...], preferred_element_type=F32)
    if has_extra:
        for s in range(e_ref.shape[0]):
            y = y + e_ref[s].astype(F32)
    o_ref[...] = x_ref[...] + g_ref[...] * y


def _proj_tile(rows, seq):
    return _div_tile(math.gcd(seq, rows), MM_TM)


def _proj_residual(a, w, layer, x, gate, extra, tile0, n_tiles, tm, seq, n_lat, batch):
    k = a.shape[1]
    d = w.shape[2]
    tn = _div_tile(d, 512)
    rows = n_tiles * tm
    rmap = _mod_row_map(tm, seq, n_lat, batch)
    in_specs = [pl.BlockSpec((tm, k), lambda i, j: (tile0 + i, 0)),
                pl.BlockSpec((None, k, tn), lambda i, j: (layer, 0, j)),
                pl.BlockSpec((tm, tn), lambda i, j: (tile0 + i, j)),
                pl.BlockSpec((None, 1, tn), lambda i, j: rmap(tile0 + i)[:2] + (j,))]
    args = [a, w, x, gate]
    n_slots = 0
    if extra is not None:
        n_slots = extra.shape[0]
        in_specs.append(pl.BlockSpec((n_slots, tm, tn), lambda i, j: (0, i, j)))
        args.append(extra)
    cost = pl.CostEstimate(flops=2 * rows * k * d, transcendentals=0,
                           bytes_accessed=rows * d * (8 + 2 * n_slots) + 2 * rows * k
                           + 2 * k * d * n_tiles)
    return pl.pallas_call(
        functools.partial(_proj_res_kernel, extra is not None),
        out_shape=jax.ShapeDtypeStruct((rows, d), F32),
        grid=(n_tiles, d // tn),
        in_specs=in_specs,
        out_specs=pl.BlockSpec((tm, tn), lambda i, j: (i, j)),
        compiler_params=_cparams("parallel", "arbitrary"),
        cost_estimate=cost,
        name="proj_residual",
    )(*args)


def _router_kernel(x_ref, w_ref, sc_ref, sh_ref, wr_ref, rb_ref, h_ref, idx_ref, gate_ref):
    x = x_ref[...]
    ms = jnp.mean(x * x, axis=-1, keepdims=True)
    h = x * lax.rsqrt(ms + NORM_EPS) * w_ref[...] * (1.0 + sc_ref[...]) + sh_ref[...]
    h_ref[...] = h.astype(h_ref.dtype)
    tm = x.shape[0]
    logits = lax.dot_general(wr_ref[...], h, NT_DIMS, precision=lax.Precision.HIGHEST,
                             preferred_element_type=F32)
    scores = _sigmoid(logits)
    choice = scores + rb_ref[...]
    per_group = N_EXPERTS // N_GROUPS
    neg = -jnp.inf
    sub = lax.broadcasted_iota(jnp.int32, (per_group, tm), 0).astype(F32)
    gs = []
    for g in range(N_GROUPS):
        xg = choice[g * per_group:(g + 1) * per_group, :]
        m1 = jnp.max(xg, axis=0, keepdims=True)
        i1 = jnp.min(jnp.where(xg == m1, sub, float(per_group)), axis=0, keepdims=True)
        m2 = jnp.max(jnp.where(sub == i1, neg, xg), axis=0, keepdims=True)
        gs.append(m1 + m2)
    sel = [jnp.zeros((1, tm), F32) for _ in range(N_GROUPS)]
    for _ in range(TOPK_GROUPS):
        m = functools.reduce(jnp.maximum, gs)
        found = jnp.zeros((1, tm), F32)
        for g in range(N_GROUPS):
            hit = jnp.where(gs[g] == m, 1.0 - found, 0.0)
            sel[g] = sel[g] + hit
            found = found + hit
            gs[g] = jnp.where(hit > 0.5, neg, gs[g])
    masked = jnp.concatenate(
        [jnp.where(sel[g] > 0.5, choice[g * per_group:(g + 1) * per_group, :], neg)
         for g in range(N_GROUPS)], axis=0)
    eidx = lax.broadcasted_iota(jnp.int32, (N_EXPERTS, tm), 0).astype(F32)
    idxs, ws = [], []
    for _ in range(TOP_K):
        m = jnp.max(masked, axis=0, keepdims=True)
        i = jnp.min(jnp.where(masked == m, eidx, float(N_EXPERTS)), axis=0, keepdims=True)
        hit = eidx == i
        ws.append(jnp.sum(jnp.where(hit, scores, 0.0), axis=0, keepdims=True))
        idxs.append(i)
        masked = jnp.where(hit, neg, masked)
    wsum = functools.reduce(lambda a, b: a + b, ws)
    idx_ref[...] = jnp.concatenate(idxs, axis=0).astype(jnp.int32)
    gate_ref[...] = jnp.concatenate(ws, axis=0) / wsum * ROUTED_SCALE


def _norm_router(x, w, sc, sh, w_router, router_bias, rows, seq, n_lat, batch):
    d = x.shape[1]
    tm = _div_tile(math.gcd(seq, rows), ROW_TILE)
    rmap = _mod_row_map(tm, seq, n_lat, batch)
    return pl.pallas_call(
        _router_kernel,
        out_shape=(jax.ShapeDtypeStruct((rows, d), BF16),
                   jax.ShapeDtypeStruct((TOP_K, rows), jnp.int32),
                   jax.ShapeDtypeStruct((TOP_K, rows), F32)),
        grid=(rows // tm,),
        in_specs=[pl.BlockSpec((tm, d), lambda i: (i, 0)),
                  pl.BlockSpec((1, d), lambda i: (0, 0)),
                  pl.BlockSpec((None, 1, d), rmap),
                  pl.BlockSpec((None, 1, d), rmap),
                  pl.BlockSpec((N_EXPERTS, d), lambda i: (0, 0)),
                  pl.BlockSpec((N_EXPERTS, 1), lambda i: (0, 0))],
        out_specs=(pl.BlockSpec((tm, d), lambda i: (i, 0)),
                   pl.BlockSpec((TOP_K, tm), lambda i: (0, i)),
                   pl.BlockSpec((TOP_K, tm), lambda i: (0, i))),
        compiler_params=_cparams("parallel"),
        name="norm_router",
    )(x, w.reshape(1, d), sc, sh, w_router.T, router_bias.reshape(N_EXPERTS, 1))


def _expert_kernel(be_ref, nu_ref, x_ref, wg_ref, wu_ref, wd_ref, wt_ref, o_ref,
                   wgb_ref, wub_ref, wdb_ref):
    i = pl.program_id(0)

    @pl.when((i == 0) | (be_ref[i] != be_ref[jnp.maximum(i - 1, 0)]))
    def _():
        wgb_ref[...] = wg_ref[...].astype(BF16)
        wub_ref[...] = wu_ref[...].astype(BF16)
        wdb_ref[...] = wd_ref[...].astype(BF16)

    @pl.when(i < nu_ref[0])
    def _():
        x = x_ref[...]
        g = jnp.dot(x, wgb_ref[...], preferred_element_type=F32)
        u = jnp.dot(x, wub_ref[...], preferred_element_type=F32)
        act = (_silu(g) * u).astype(BF16)
        y = jnp.dot(act, wdb_ref[...], preferred_element_type=F32)
        o_ref[...] = (y * wt_ref[...]).astype(o_ref.dtype)

    @pl.when(i >= nu_ref[0])
    def _():
        o_ref[...] = jnp.zeros_like(o_ref)


def _experts(xs, w_gate, w_up, w_down, layer, buf_w, block_expert, n_used, bm):
    rows, d = xs.shape
    nb = rows // bm
    ff = w_gate.shape[3]
    grid_spec = pltpu.PrefetchScalarGridSpec(
        num_scalar_prefetch=2,
        grid=(nb,),
        in_specs=[pl.BlockSpec((bm, d), lambda i, be, nu: (i, 0)),
                  pl.BlockSpec((None, None, d, ff), lambda i, be, nu: (layer, be[i], 0, 0)),
                  pl.BlockSpec((None, None, d, ff), lambda i, be, nu: (layer, be[i], 0, 0)),
                  pl.BlockSpec((None, None, ff, d), lambda i, be, nu: (layer, be[i], 0, 0)),
                  pl.BlockSpec((bm, 1), lambda i, be, nu: (i, 0))],
        out_specs=pl.BlockSpec((bm, d), lambda i, be, nu: (i, 0)),
        scratch_shapes=[pltpu.VMEM((d, ff), BF16), pltpu.VMEM((d, ff), BF16),
                        pltpu.VMEM((ff, d), BF16)],
    )
    cost = pl.CostEstimate(flops=6 * rows * d * ff, transcendentals=rows * ff,
                           bytes_accessed=4 * rows * d + 12 * N_EXPERTS * d * ff)
    return pl.pallas_call(
        _expert_kernel,
        out_shape=jax.ShapeDtypeStruct((rows, d), BF16),
        grid_spec=grid_spec,
        compiler_params=_cparams("arbitrary"),
        cost_estimate=cost,
        name="routed_experts",
    )(block_expert, n_used, xs, w_gate, w_up, w_down, buf_w.reshape(rows, 1))


def _take_rows(arr, idx):
    return arr.at[idx].get(mode="promise_in_bounds")


def _dispatch_plan(idx_t, gate_t, bm):
    k, t = idx_t.shape
    a = k * t
    experts = jnp.arange(N_EXPERTS, dtype=jnp.int32)
    flat_e = idx_t.reshape(a)
    gates = gate_t.reshape(a)
    iota = jnp.arange(a, dtype=jnp.int32)
    _, order = lax.sort_key_val(flat_e, iota)
    _, inv = lax.sort_key_val(order, iota)
    counts = jnp.sum((flat_e[:, None] == experts[None, :]).astype(jnp.int32), axis=0)
    starts = jnp.cumsum(counts) - counts
    padded = (counts + bm - 1) // bm * bm
    padded_end = jnp.cumsum(padded)
    padded_start = padded_end - padded
    nb = -(-a // bm) + N_EXPERTS
    n_used = (padded_end[-1:] // bm).astype(jnp.int32)
    blk = jnp.arange(nb, dtype=jnp.int32)
    block_expert = jnp.minimum(
        jnp.sum((blk[:, None] * bm >= padded_end[None, :]).astype(jnp.int32), axis=1), N_EXPERTS - 1)
    onehot = (block_expert[:, None] == experts[None, :]).astype(jnp.int32)
    b_pstart, b_count, b_start = [jnp.sum(onehot * v[None, :], axis=1)
                                  for v in (padded_start, counts, starts)]
    rank = blk[:, None] * bm + jnp.arange(bm, dtype=jnp.int32)[None, :] - b_pstart[:, None]
    valid = (rank < b_count[:, None]).reshape(nb * bm)
    src = jnp.clip(b_start[:, None] + rank, 0, a - 1).reshape(nb * bm)
    asg = _take_rows(order, src)
    buf_tok = jnp.where(valid, asg % t, 0)
    buf_w = jnp.where(valid, _take_rows(gates, asg), 0.0)
    shift = padded_start - starts
    pos = inv + _take_rows(shift, flat_e)
    return buf_tok, buf_w, pos, block_expert.astype(jnp.int32), n_used


def _swiglu_up_kernel(a_ref, wg_ref, wu_ref, o_ref):
    a = a_ref[...]
    g = jnp.dot(a, wg_ref[...], preferred_element_type=F32)
    u = jnp.dot(a, wu_ref[...], preferred_element_type=F32)
    o_ref[...] = (_silu(g) * u).astype(o_ref.dtype)


def _swiglu_up(a, wg, wu, layer):
    m, k = a.shape
    n = wg.shape[2]
    tm = _div_tile(m, MM_TM)
    tn = _div_tile(n, 512)
    wsp = pl.BlockSpec((None, k, tn), lambda i, j: (layer, 0, j))
    return pl.pallas_call(
        _swiglu_up_kernel,
        out_shape=jax.ShapeDtypeStruct((m, n), BF16),
        grid=(m // tm, n // tn),
        in_specs=[pl.BlockSpec((tm, k), lambda i, j: (i, 0)), wsp, wsp],
        out_specs=pl.BlockSpec((tm, tn), lambda i, j: (i, j)),
        compiler_params=_cparams("parallel", "arbitrary"),
        name="shared_swiglu_up",
    )(a, wg, wu)


def _lambda_init_for(layer):
    return 0.8 - 0.6 * math.exp(-0.3 * layer)


def kernel(x, c, ctx, c_ctx, w_ada, b_ada, norm_mix, norm_ffn, w_in, hgrn_lb, hgrn_norm, diff_lambda, diff_subln, w_branch, w_mgate, b_mgate, w_out, w_router, router_bias, w_exp_gate, w_exp_up, w_exp_down, w_sh_gate, w_sh_up, w_sh_down, norm_final):
    batch, seq, d = x.shape
    ctx_len = ctx.shape[1]
    depth = w_ada.shape[0]
    n_lat, n_ctx = batch * seq, batch * ctx_len
    n_all = n_lat + n_ctx
    assert batch < 8 and seq % GRID_W == 0 and ctx_len % 8 == 0

    p_lb = jax.nn.softmax(hgrn_lb.astype(F32), axis=1)
    lower_bounds = jnp.cumsum(p_lb, axis=1) - p_lb[:, :1]

    c8 = jnp.zeros((8, d), F32).at[:batch].set(c).at[batch].set(c_ctx)
    mod = _ada_mod(c8, w_ada, b_ada)

    rope_tm = _div_tile(math.gcd(seq, n_ctx), ROW_TILE)
    rope_tabs = _rope_tables(seq, rope_tm)
    hg_tabs = _hgrn_tables(_div_tile(math.gcd(seq, ctx_len), HG_CHUNK))
    rt_tabs = _ret_tables(_div_tile(math.gcd(seq, ctx_len), RT_CHUNK))

    w_in_b, w_branch_b, w_out_b = [w.astype(BF16) for w in (w_in, w_branch, w_out)]
    w_mgate8, w_mgate_scale = _quantize_cols(w_mgate)
    w_sg_b, w_su_b, w_sd_b = [w.astype(BF16) for w in (w_sh_gate, w_sh_up, w_sh_down)]

    xa = jnp.concatenate([x.reshape(n_lat, d), ctx.reshape(n_ctx, d)], axis=0)
    for l in range(depth):
        need_ctx = l < depth - 1
        rows = n_all if need_ctx else n_lat
        sh1, sc1, g1, sh2, sc2, g2 = [mod[l, :, i * d:(i + 1) * d].reshape(8, 1, d) for i in range(6)]

        h, h8, h_scale = _norm_mod(xa, norm_mix[l], sc1, sh1, n_all, seq, n_lat, batch, BF16,
                                   quantize=True)
        p = _matmul(h, w_in_b, l, BF16)
        qd, kd, qr, kr = _rope_prep(p, rope_tabs, rope_tm, seq, n_lat)
        y_hg, yc_hg = _hgrn(p, lower_bounds[0, l], lower_bounds[1, l], hgrn_norm[l], hg_tabs,
                            batch, seq, ctx_len, need_ctx)
        lam_init = _lambda_init_for(l)
        y_da = _diff_attn(qd, kd, p, diff_lambda[l], diff_subln[l], lam_init, batch, seq, ctx_len, True)
        y_rt, yc_rt = _retention(qr, kr, p, rt_tabs, batch, seq, ctx_len, need_ctx)
        ys = [y_hg, y_da, y_rt]
        if need_ctx:
            yc_da = _diff_attn(qd, kd, p, diff_lambda[l], diff_subln[l], lam_init, batch, seq,
                               ctx_len, False)
            ys = [jnp.concatenate([a, b], axis=0) for a, b in zip(ys, (yc_hg, yc_da, yc_rt))]
        merged = _merge(h8, h_scale, ys, w_mgate8, w_mgate_scale, b_mgate, w_branch_b, l, rows)
        tm = _proj_tile(rows, seq)
        n_tiles = rows // tm
        xa = _proj_residual(merged, w_out_b, l, xa, g1, None, 0, n_tiles, tm, seq, n_lat, batch)

        h2, idx_t, gate_t = _norm_router(xa, norm_ffn[l], sc2, sh2, w_router[l], router_bias[l],
                                         rows, seq, n_lat, batch)
        up = _swiglu_up(h2, w_sg_b, w_su_b, l)
        bounds = [n_tiles * g // MOE_GROUPS for g in range(MOE_GROUPS + 1)]
        groups = [(t0, t1 - t0) for t0, t1 in zip(bounds[:-1], bounds[1:]) if t1 > t0]
        plans, gathered, y_slots = [], [], []
        for t0, nt in groups:
            cols = slice(t0 * tm, (t0 + nt) * tm)
            plans.append(_dispatch_plan(idx_t[:, cols], gate_t[:, cols], MOE_BM))
        for (t0, nt), plan in zip(groups, plans):
            gathered.append(jnp.take(h2, plan[0] + t0 * tm, axis=0, mode="clip"))
        for (t0, nt), plan, xs in zip(groups, plans, gathered):
            buf_tok, buf_w, pos, block_expert, n_used = plan
            y_sorted = _experts(xs, w_exp_gate, w_exp_up, w_exp_down, l, buf_w, block_expert,
                                n_used, MOE_BM)
            y_slots.append(jnp.take(y_sorted, pos, axis=0, mode="clip").reshape(TOP_K, nt * tm, d))
        parts = [_proj_residual(up, w_sd_b, l, xa, g2, ys_g, t0, nt, tm, seq, n_lat, batch)
                 for (t0, nt), ys_g in zip(groups, y_slots)]
        xa = parts[0] if len(parts) == 1 else jnp.concatenate(parts, axis=0)

    zeros = jnp.zeros((8, 1, d), F32)
    out = _norm_mod(xa, norm_final, zeros, zeros, n_lat, seq, n_lat, batch, F32)
    return out.reshape(batch, seq, d)
```
